```python
import jax, jax.numpy as jnp
from jax import lax
import numpy as np

D_MODEL = 1024
BATCH = 2
SEQ = 8192
DEPTH = 2
DEC_BATCH = 128
DEC_SEQ = 1
PAST_LEN = 16384
PAGE_SIZE = 128

CONV_GROUPS = 4
CONV_GROUP_W = 64
CONV_W = CONV_GROUPS * CONV_GROUP_W
CONV_K = 3
MLA_HEADS = 8
NOPE_D = 64
ROPE_D = 32
QK_D = NOPE_D + ROPE_D
V_D = 64
MLA_W = MLA_HEADS * V_D
Q_RANK = 384
KV_RANK = 128
ROPE_THETA = 10000.0
Q_BLOCK = 128
ATTN_SCALE = QK_D ** -0.5
ML_HEADS = 4
ML_D = 64
ML_W = ML_HEADS * ML_D
ML_CHUNK = 64
MIX_W = CONV_W + MLA_W + ML_W
IN_W = 3 * CONV_W + Q_RANK + KV_RANK + ROPE_D + 4 * ML_W + 2 * ML_HEADS
D_FF = 2816
N_EXPERTS = 8
TOP_K = 2
D_FF_EXPERT = 1408
MOE_BLOCK = 128
RMS_EPS = 1e-6

kernel_name = 'hybrid_conv_mla_mlstm_decode_step'


def rmsnorm(x, g):
    xf = x.astype(jnp.float32)
    y = xf * lax.rsqrt(jnp.mean(xf * xf, axis=-1, keepdims=True) + RMS_EPS)
    return (y * g.astype(jnp.float32)).astype(x.dtype)


def qk_gain(g):
    return jnp.concatenate([g[:NOPE_D], g[NOPE_D:], g[NOPE_D:]])


def rope(x, pos):
    half = ROPE_D // 2
    inv_freq = ROPE_THETA ** (-jnp.arange(half, dtype=jnp.float32) / half)
    ang = pos.astype(jnp.float32)[:, None] * inv_freq[None, :]
    ang = ang.reshape(ang.shape[:1] + (1,) * (x.ndim - 3) + (half,))
    cos, sin = jnp.cos(ang), jnp.sin(ang)
    xf = x.astype(jnp.float32)
    x1, x2 = xf[..., :half], xf[..., half:]
    return jnp.concatenate([x1 * cos - x2 * sin, x2 * cos + x1 * sin], axis=-1).astype(x.dtype)


def split_in(h):
    sizes = [CONV_W, CONV_W, CONV_W, Q_RANK, KV_RANK, ROPE_D, ML_W, ML_W, ML_W, ML_W, ML_HEADS, ML_HEADS]
    return jnp.split(h, [int(i) for i in np.cumsum(sizes)[:-1]], axis=-1)


def project(x, g_norm, w_in, g_cq, w_uq, g_ckv, g_q, b_i, b_f, pos):
    bn, L = x.shape[:2]
    xn = rmsnorm(x, g_norm)
    cb, cc, ch, cq, ckv, kr, mq, mk, mv, mo, mi, mf = split_in(xn @ w_in)
    q = (rmsnorm(cq, g_cq) @ w_uq).reshape(bn, L, MLA_HEADS, QK_D)
    q = jnp.concatenate([q[..., :NOPE_D], rope(q[..., NOPE_D:], pos)], axis=-1)
    q = rmsnorm(q, qk_gain(g_q))
    ckv = rmsnorm(ckv, g_ckv)
    kr = rope(kr, pos)
    mq = mq.reshape(bn, L, ML_HEADS, ML_D)
    mk = mk.reshape(bn, L, ML_HEADS, ML_D) * (ML_D ** -0.5)
    mv = mv.reshape(bn, L, ML_HEADS, ML_D)
    log_i = (mi + b_i).astype(jnp.float32)
    log_f = jax.nn.log_sigmoid((mf + b_f).astype(jnp.float32))
    return cb, cc * ch, q, ckv, kr, mq, mk, mv, mo, log_i, log_f


def short_conv(u, prefix, w):
    L = u.shape[1]
    full = jnp.concatenate([prefix.astype(u.dtype), u], axis=1)
    out = sum(w[j] * full[:, j:j + L] for j in range(CONV_K))
    return out, full[:, L:]


def mla_prompt(q, ckv, kr, w_uk, w_uv, g_k):
    bn, s_len = q.shape[:2]
    k_nope = jnp.einsum('bsr,rhd->bshd', ckv, w_uk)
    k = jnp.concatenate([k_nope, jnp.broadcast_to(kr[:, :, None, :], (bn, s_len, MLA_HEADS, ROPE_D))], axis=-1)
    k = rmsnorm(k, qk_gain(g_k))
    v = jnp.einsum('bsr,rhd->bshd', ckv, w_uv)
    n_blk = s_len // Q_BLOCK
    q_blocks = q.reshape(bn, n_blk, Q_BLOCK, MLA_HEADS, QK_D).swapaxes(0, 1)
    k_pos = jnp.arange(s_len)

    def attend_block(args):
        q_blk, i = args
        s = jnp.einsum('bqhd,bkhd->bhqk', q_blk, k, preferred_element_type=jnp.float32) * ATTN_SCALE
        q_pos = i * Q_BLOCK + jnp.arange(Q_BLOCK)
        s = jnp.where(k_pos[None, :] <= q_pos[:, None], s, -jnp.inf)
        p = jax.nn.softmax(s, axis=-1).astype(v.dtype)
        return jnp.einsum('bhqk,bkhd->bqhd', p, v)

    o = lax.map(attend_block, (q_blocks, jnp.arange(n_blk)))
    return o.swapaxes(0, 1).reshape(bn, s_len, MLA_HEADS, V_D)


def mla_sample(q, ckv_new, kr_new, cache_ckv, cache_krope, page_table, layer, w_uk, w_uv, g_k):
    past_len = page_table.shape[1] * PAGE_SIZE
    L = q.shape[1]
    qf = q * qk_gain(g_k)
    t_pos = jnp.arange(past_len + L)
    valid = t_pos[None, :] <= past_len + jnp.arange(L)[:, None]

    def one_sequence(args):
        pt, qs, cn, kn = args
        c = jnp.concatenate([cache_ckv[layer, pt].reshape(past_len, KV_RANK).astype(cn.dtype), cn], axis=0)
        krs = jnp.concatenate([cache_krope[layer, pt].reshape(past_len, ROPE_D).astype(kn.dtype), kn], axis=0)
        k_nope = jnp.einsum('tr,rhd->thd', c, w_uk)
        ssq = jnp.sum(jnp.square(k_nope.astype(jnp.float32)), -1) + jnp.sum(jnp.square(krs.astype(jnp.float32)), -1)[:, None]
        inv = lax.rsqrt(ssq / QK_D + RMS_EPS).T
        s = (jnp.einsum('qhd,thd->hqt', qs[..., :NOPE_D], k_nope, preferred_element_type=jnp.float32)
             + jnp.einsum('qhd,td->hqt', qs[..., NOPE_D:], krs, preferred_element_type=jnp.float32))
        s = jnp.where(valid[None], s * inv[:, None, :] * ATTN_SCALE, -jnp.inf)
        p = jax.nn.softmax(s, axis=-1).astype(c.dtype)
        pc = jnp.einsum('hqt,tr->qhr', p, c)
        return jnp.einsum('qhr,rhd->qhd', pc, w_uv)

    return lax.map(one_sequence, (page_table, qf, ckv_new, kr_new))


def mlstm_chunk(carry, xs):
    C0, n0, m0 = (a.astype(jnp.float32) for a in carry)
    q, k, v, log_i, log_f = (a.astype(jnp.float32) for a in xs)
    L = q.shape[1]
    b = jnp.cumsum(log_f, axis=1).swapaxes(1, 2)
    li = log_i.swapaxes(1, 2)
    causal = jnp.tril(jnp.ones((L, L), dtype=bool))
    dmat = jnp.where(causal, b[..., :, None] - b[..., None, :] + li[..., None, :], -jnp.inf)
    a = b + m0[..., None]
    m = jnp.maximum(a, jnp.max(dmat, axis=-1))
    w = jnp.exp(dmat - m[..., None])
    ws = jnp.exp(a - m)
    s = jnp.einsum('bthd,bshd->bhts', q, k) * w
    num = jnp.einsum('bhts,bshd->bthd', s, v) + jnp.einsum('bhvk,bthk->bthv', C0, q) * ws.swapaxes(1, 2)[..., None]
    den = jnp.sum(s, axis=-1) + ws * jnp.einsum('bhk,bthk->bht', n0, q)
    h = num / jnp.maximum(jnp.abs(den), jnp.exp(-m)).swapaxes(1, 2)[..., None]
    m_end = m[..., -1]
    w_end = jnp.exp(b[..., -1:] - b + li - m_end[..., None])
    decay = jnp.exp(b[..., -1] + m0 - m_end)
    C1 = decay[..., None, None] * C0 + jnp.einsum('bhs,bshv,bshk->bhvk', w_end, v, k)
    n1 = decay[..., None] * n0 + jnp.einsum('bhs,bshk->bhk', w_end, k)
    return (C1, n1, m_end), h


def mlstm_prompt(q, k, v, log_i, log_f):
    bn, s_len = q.shape[:2]
    nc = s_len // ML_CHUNK

    def to_chunks(a):
        return a.reshape((bn, nc, ML_CHUNK) + a.shape[2:]).swapaxes(0, 1)

    init = (jnp.zeros((bn, ML_HEADS, ML_D, ML_D), jnp.float32),
            jnp.zeros((bn, ML_HEADS, ML_D), jnp.float32),
            jnp.zeros((bn, ML_HEADS), jnp.float32))
    state, h = lax.scan(mlstm_chunk, init, (to_chunks(q), to_chunks(k), to_chunks(v), to_chunks(log_i), to_chunks(log_f)))
    return state, h.swapaxes(0, 1).reshape(bn, s_len, ML_HEADS, ML_D)


def merge(cb, conv_out, o_mla, h_ml, mo, g_ml, w_out):
    bn, L = cb.shape[:2]
    ml = rmsnorm(h_ml, g_ml).reshape(bn, L, ML_W).astype(cb.dtype) * jax.nn.sigmoid(mo)
    mix = jnp.concatenate([cb * conv_out, o_mla.reshape(bn, L, MLA_W).astype(cb.dtype), ml], axis=-1)
    return mix @ w_out


def swiglu(x, w1, w3, w2):
    return (jax.nn.silu(x @ w1) * (x @ w3)) @ w2


def moe_ffn(x, w_router, w1, w3, w2):
    shp = x.shape
    xt = x.reshape(-1, shp[-1])
    T = xt.shape[0]
    logits = jnp.einsum('td,de->te', xt, w_router, preferred_element_type=jnp.float32)
    top_v, top_i = lax.top_k(logits, TOP_K)
    gates = jax.nn.softmax(top_v, axis=-1)
    A = T * TOP_K
    e_flat = top_i.reshape(-1)
    t_flat = jnp.repeat(jnp.arange(T, dtype=jnp.int32), TOP_K)
    g_flat = gates.reshape(-1)
    order = jnp.argsort(e_flat)
    e_s, t_s, g_s = e_flat[order], t_flat[order], g_flat[order]
    counts = jnp.bincount(e_flat, length=N_EXPERTS)
    start = jnp.cumsum(counts) - counts
    padded = (counts + MOE_BLOCK - 1) // MOE_BLOCK * MOE_BLOCK
    pad_end = jnp.cumsum(padded)
    pad_start = pad_end - padded
    dest = pad_start[e_s] + jnp.arange(A) - start[e_s]
    n_blk = (A + N_EXPERTS * (MOE_BLOCK - 1) + MOE_BLOCK - 1) // MOE_BLOCK
    n_slot = n_blk * MOE_BLOCK
    slot_tok = jnp.zeros((n_slot,), jnp.int32).at[dest].set(t_s)
    slot_gate = jnp.zeros((n_slot,), jnp.float32).at[dest].set(g_s)
    blk_exp = jnp.minimum(jnp.searchsorted(pad_end, jnp.arange(n_blk) * MOE_BLOCK, side='right'), N_EXPERTS - 1)

    def expert_block(args):
        toks, e = args
        return swiglu(xt[toks], w1[e], w3[e], w2[e])

    yb = lax.map(expert_block, (slot_tok.reshape(n_blk, MOE_BLOCK), blk_exp))
    y = jnp.zeros_like(xt).at[slot_tok].add(yb.reshape(n_slot, -1) * slot_gate[:, None].astype(xt.dtype))
    return y.reshape(shp)


def setup_inputs(seed: int = 0) -> dict:
    key = jax.random.key(seed)
    ks = iter(jax.random.split(key, 48))
    n_pages = PAST_LEN // PAGE_SIZE
    n_pool = (DEC_BATCH * n_pages * 5) // 4
    n_dense = (DEPTH + 1) // 2
    n_moe = DEPTH // 2

    def nrm(shape, scale):
        return jax.random.normal(next(ks), shape, jnp.float32) * scale

    def gain(shape):
        return 1.0 + nrm(shape, 0.05)

    x_prompt = nrm((BATCH, SEQ, D_MODEL), 1.0)
    x_sample = nrm((DEC_BATCH, DEC_SEQ, D_MODEL), 1.0)
    cache_ckv = nrm((DEPTH, n_pool, PAGE_SIZE, KV_RANK), 1.0)
    cache_krope = nrm((DEPTH, n_pool, PAGE_SIZE, ROPE_D), 1.0)
    page_table = jax.random.permutation(next(ks), n_pool)[:DEC_BATCH * n_pages].reshape(DEC_BATCH, n_pages).astype(jnp.int32)
    state_conv = nrm((DEPTH, DEC_BATCH, CONV_K - 1, CONV_W), 1.0)
    state_C = nrm((DEPTH, DEC_BATCH, ML_HEADS, ML_D, ML_D), 0.3)
    state_n = nrm((DEPTH, DEC_BATCH, ML_HEADS, ML_D), 0.3)
    state_m = nrm((DEPTH, DEC_BATCH, ML_HEADS), 1.0)
    return {
        'x_prompt': x_prompt, 'x_sample': x_sample,
        'cache_ckv': cache_ckv, 'cache_krope': cache_krope, 'page_table': page_table,
        'state_conv': state_conv, 'state_C': state_C, 'state_n': state_n, 'state_m': state_m,
        'g_norm_mix': gain((DEPTH, D_MODEL)),
        'g_norm_ffn': gain((DEPTH, D_MODEL)),
        'w_in': nrm((DEPTH, D_MODEL, IN_W), D_MODEL ** -0.5),
        'conv_w': nrm((DEPTH, CONV_K, CONV_W), CONV_K ** -0.5),
        'g_cq': gain((DEPTH, Q_RANK)),
        'w_uq': nrm((DEPTH, Q_RANK, MLA_HEADS * QK_D), Q_RANK ** -0.5),
        'g_ckv': gain((DEPTH, KV_RANK)),
        'w_uk': nrm((DEPTH, KV_RANK, MLA_HEADS, NOPE_D), KV_RANK ** -0.5),
        'w_uv': nrm((DEPTH, KV_RANK, MLA_HEADS, V_D), KV_RANK ** -0.5),
        'g_q': gain((DEPTH, NOPE_D + ROPE_D // 2)),
        'g_k': gain((DEPTH, NOPE_D + ROPE_D // 2)),
        'b_i': nrm((DEPTH, ML_HEADS), 0.1),
        'b_f': 3.0 + nrm((DEPTH, ML_HEADS), 0.5),
        'g_ml': gain((DEPTH, ML_HEADS, ML_D)),
        'w_out': nrm((DEPTH, MIX_W, D_MODEL), MIX_W ** -0.5),
        'w_ff1': nrm((n_dense, D_MODEL, D_FF), D_MODEL ** -0.5),
        'w_ff3': nrm((n_dense, D_MODEL, D_FF), D_MODEL ** -0.5),
        'w_ff2': nrm((n_dense, D_FF, D_MODEL), D_FF ** -0.5),
        'w_router': nrm((n_moe, D_MODEL, N_EXPERTS), D_MODEL ** -0.5),
        'w_e1': nrm((n_moe, N_EXPERTS, D_MODEL, D_FF_EXPERT), D_MODEL ** -0.5),
        'w_e3': nrm((n_moe, N_EXPERTS, D_MODEL, D_FF_EXPERT), D_MODEL ** -0.5),
        'w_e2': nrm((n_moe, N_EXPERTS, D_FF_EXPERT, D_MODEL), D_FF_EXPERT ** -0.5),
    }


def reference(x_prompt, x_sample, cache_ckv, cache_krope, page_table, state_conv, state_C, state_n, state_m,
              g_norm_mix, g_norm_ffn, w_in, conv_w, g_cq, w_uq, g_ckv, w_uk, w_uv, g_q, g_k, b_i, b_f, g_ml,
              w_out, w_ff1, w_ff3, w_ff2, w_router, w_e1, w_e3, w_e2):
    past_len = page_table.shape[1] * PAGE_SIZE
    pos_p = jnp.arange(x_prompt.shape[1])
    pos_s = past_len + jnp.arange(x_sample.shape[1])
    xp, xs = x_prompt, x_sample
    ckv_p, ckv_s, kr_p, kr_s, cv_p, cv_s = [], [], [], [], [], []
    C_p, C_s, n_p, n_s, m_p, m_s = [], [], [], [], [], []
    for l in range(DEPTH):
        cb, u, q, ckv, kr, mq, mk, mv, mo, li, lf = project(xp, g_norm_mix[l], w_in[l], g_cq[l], w_uq[l], g_ckv[l], g_q[l], b_i[l], b_f[l], pos_p)
        conv_out, conv_st = short_conv(u, jnp.zeros((xp.shape[0], CONV_K - 1, CONV_W), u.dtype), conv_w[l])
        o_mla = mla_prompt(q, ckv, kr, w_uk[l], w_uv[l], g_k[l])
        (C1, n1, m1), h_ml = mlstm_prompt(mq, mk, mv, li, lf)
        xp = xp + merge(cb, conv_out, o_mla, h_ml, mo, g_ml[l], w_out[l])
        ckv_p.append(ckv); kr_p.append(kr); cv_p.append(conv_st); C_p.append(C1); n_p.append(n1); m_p.append(m1)
        cb, u, q, ckv, kr, mq, mk, mv, mo, li, lf = project(xs, g_norm_mix[l], w_in[l], g_cq[l], w_uq[l], g_ckv[l], g_q[l], b_i[l], b_f[l], pos_s)
        conv_out, conv_st = short_conv(u, state_conv[l], conv_w[l])
        o_mla = mla_sample(q, ckv, kr, cache_ckv, cache_krope, page_table, l, w_uk[l], w_uv[l], g_k[l])
        (C1, n1, m1), h_ml = mlstm_chunk((state_C[l], state_n[l], state_m[l]), (mq, mk, mv, li, lf))
        xs = xs + merge(cb, conv_out, o_mla, h_ml, mo, g_ml[l], w_out[l])
        ckv_s.append(ckv); kr_s.append(kr); cv_s.append(conv_st); C_s.append(C1); n_s.append(n1); m_s.append(m1)
        j = l // 2
        if l % 2 == 0:
            xp = xp + swiglu(rmsnorm(xp, g_norm_ffn[l]), w_ff1[j], w_ff3[j], w_ff2[j])
            xs = xs + swiglu(rmsnorm(xs, g_norm_ffn[l]), w_ff1[j], w_ff3[j], w_ff2[j])
        else:
            xp = xp + moe_ffn(rmsnorm(xp, g_norm_ffn[l]), w_router[j], w_e1[j], w_e3[j], w_e2[j])
            xs = xs + moe_ffn(rmsnorm(xs, g_norm_ffn[l]), w_router[j], w_e1[j], w_e3[j], w_e2[j])
    return (xp, xs,
            jnp.stack(ckv_p), jnp.stack(ckv_s), jnp.stack(kr_p), jnp.stack(kr_s),
            jnp.stack(cv_p), jnp.stack(cv_s), jnp.stack(C_p), jnp.stack(C_s),
            jnp.stack(n_p), jnp.stack(n_s), jnp.stack(m_p), jnp.stack(m_s))
```

```python
import functools
import math

import jax
import jax.numpy as jnp
import numpy as np
from jax import lax
from jax.experimental import pallas as pl
from jax.experimental.pallas import tpu as pltpu

F32 = jnp.float32
BF16 = jnp.bfloat16

D_MODEL = 1024
PAGE_SIZE = 128
CONV_W = 256
CONV_K = 3
MLA_HEADS = 8
NOPE_D = 64
ROPE_D = 32
HALF_ROPE = ROPE_D // 2
QK_D = NOPE_D + ROPE_D
V_D = 64
MLA_W = MLA_HEADS * V_D
Q_RANK = 384
KV_RANK = 128
ROPE_THETA = 10000.0
ATTN_SCALE = QK_D ** -0.5
ML_HEADS = 4
ML_D = 64
ML_W = ML_HEADS * ML_D
D_FF = 2816
N_EXPERTS = 8
TOP_K = 2
D_FF_EXPERT = 1408
RMS_EPS = 1e-6
LOG2E = math.log2(math.e)

LANE = 128
HP = LANE
IN_MAIN = 2304
IN_PAD = IN_MAIN + LANE
GATE_I = ROPE_D
GATE_F = ROPE_D + ML_HEADS
VMEM_LIMIT = 56 * 1024 * 1024

ATTN_T = 256
ML_L = 128
DEC_PAGES = 16
NEW_ROWS = 16
MOE_TM = 512


def _cparams(sem):
    return pltpu.CompilerParams(dimension_semantics=sem, vmem_limit_bytes=VMEM_LIMIT)


def _rms(x, g):
    return x * lax.rsqrt(jnp.mean(x * x, axis=-1, keepdims=True) + RMS_EPS) * g


def _dot(a, b):
    return jnp.dot(a, b, preferred_element_type=F32)


def _dot_nt(a, b):
    return lax.dot_general(a, b, (((1,), (1,)), ((), ())), preferred_element_type=F32)


def _split2(x):
    hi = x.astype(BF16)
    lo = (x - hi.astype(F32)).astype(BF16)
    return hi, lo


def _split3(x):
    hi = x.astype(BF16)
    r = x - hi.astype(F32)
    mid = r.astype(BF16)
    lo = (r - mid.astype(F32)).astype(BF16)
    return hi, mid, lo


def _inproj_kernel(x_ref, g_ref, w_ref, o_ref):
    xn = _rms(x_ref[...], g_ref[...]).astype(BF16)
    o_ref[...] = _dot(xn, w_ref[...])


def inproj(x, g, w, tm):
    m = x.shape[0]
    n = w.shape[1]
    return pl.pallas_call(
        _inproj_kernel,
        grid=(m // tm,),
        in_specs=[pl.BlockSpec((tm, D_MODEL), lambda i: (i, 0)),
                  pl.BlockSpec((1, D_MODEL), lambda i: (0, 0)),
                  pl.BlockSpec((D_MODEL, n), lambda i: (0, 0))],
        out_specs=pl.BlockSpec((tm, n), lambda i: (i, 0)),
        out_shape=jax.ShapeDtypeStruct((m, n), F32),
        compiler_params=_cparams(("parallel",)),
        name="inproj",
    )(x, g, w)


def _q_kernel(cq_ref, gcq_ref, wuq_ref, gq_ref, c_ref, s1_ref, s2_ref, o_ref):
    cqn = _rms(cq_ref[...], gcq_ref[...]).astype(BF16)
    q = _dot(cqn, wuq_ref[...])
    c, s1, s2, gq = c_ref[...], s1_ref[...], s2_ref[...], gq_ref[...]
    for h in range(MLA_HEADS):
        qh = q[:, h * HP:(h + 1) * HP]
        qh = qh * c + pltpu.roll(qh, HP - HALF_ROPE, 1) * s1 + pltpu.roll(qh, HALF_ROPE, 1) * s2
        ssq = jnp.sum(qh * qh, axis=-1, keepdims=True)
        qh = qh * lax.rsqrt(ssq * (1.0 / QK_D) + RMS_EPS) * gq
        o_ref[:, h * HP:(h + 1) * HP] = qh.astype(o_ref.dtype)


def q_path(h, gcq, wuq, gq, tabs, tm, n_pos_blk):
    m = h.shape[0]
    cq_blk = (3 * CONV_W) // Q_RANK
    tab_spec = pl.BlockSpec((tm, HP), lambda i: (i % n_pos_blk, 0))
    return pl.pallas_call(
        _q_kernel,
        grid=(m // tm,),
        in_specs=[pl.BlockSpec((tm, Q_RANK), lambda i: (i, cq_blk)),
                  pl.BlockSpec((1, Q_RANK), lambda i: (0, 0)),
                  pl.BlockSpec((Q_RANK, MLA_HEADS * HP), lambda i: (0, 0)),
                  pl.BlockSpec((1, HP), lambda i: (0, 0)),
                  tab_spec, tab_spec, tab_spec],
        out_specs=pl.BlockSpec((tm, MLA_HEADS * HP), lambda i: (i, 0)),
        out_shape=jax.ShapeDtypeStruct((m, MLA_HEADS * HP), BF16),
        compiler_params=_cparams(("parallel",)),
        name="q_path",
    )(h, gcq, wuq, gq, *tabs)


def _ckv_kernel(ckv_ref, kr_ref, g_ref, c_ref, s1_ref, s2_ref, ckvn_ref, krr_ref):
    ckvn_ref[...] = _rms(ckv_ref[...], g_ref[...])
    kr = kr_ref[...]
    krr_ref[...] = (kr * c_ref[...] + pltpu.roll(kr, LANE - HALF_ROPE, 1) * s1_ref[...]
                    + pltpu.roll(kr, HALF_ROPE, 1) * s2_ref[...])


def ckv_path(h, g_ckv, tabs, tm, n_pos_blk):
    m = h.shape[0]
    ckv_blk = (3 * CONV_W + Q_RANK) // KV_RANK
    kr_blk = IN_MAIN // LANE
    tab_spec = pl.BlockSpec((tm, LANE), lambda i: (i % n_pos_blk, 0))
    return pl.pallas_call(
        _ckv_kernel,
        grid=(m // tm,),
        in_specs=[pl.BlockSpec((tm, KV_RANK), lambda i: (i, ckv_blk)),
                  pl.BlockSpec((tm, LANE), lambda i: (i, kr_blk)),
                  pl.BlockSpec((1, KV_RANK), lambda i: (0, 0)),
                  tab_spec, tab_spec, tab_spec],
        out_specs=[pl.BlockSpec((tm, KV_RANK), lambda i: (i, 0)),
                   pl.BlockSpec((tm, LANE), lambda i: (i, 0))],
        out_shape=[jax.ShapeDtypeStruct((m, KV_RANK), F32),
                   jax.ShapeDtypeStruct((m, LANE), F32)],
        compiler_params=_cparams(("parallel",)),
        name="ckv_path",
    )(h, h, g_ckv, *tabs)


def _kv_kernel(ckvn_ref, krr_ref, wk_ref, wvt_ref, gk_ref, k_ref, vt_ref):
    cb = ckvn_ref[...].astype(BF16)
    ck = jnp.concatenate([cb, krr_ref[...].astype(BF16)], axis=1)
    k = _dot(ck, wk_ref[...])
    gk = gk_ref[...]
    for h in range(MLA_HEADS):
        kh = k[:, h * HP:(h + 1) * HP]
        ssq = jnp.sum(kh * kh, axis=-1, keepdims=True)
        k_ref[:, h * HP:(h + 1) * HP] = (kh * lax.rsqrt(ssq * (1.0 / QK_D) + RMS_EPS) * gk).astype(BF16)
    vt_ref[...] = _dot_nt(wvt_ref[...], cb).astype(BF16)


def kv_path(ckvn, krr, wk, wvt, gk, batch, seq):
    tm = ATTN_T
    m = ckvn.shape[0]
    nb = seq // tm
    return pl.pallas_call(
        _kv_kernel,
        grid=(m // tm,),
        in_specs=[pl.BlockSpec((tm, KV_RANK), lambda i: (i, 0)),
                  pl.BlockSpec((tm, LANE), lambda i: (i, 0)),
                  pl.BlockSpec((2 * LANE, MLA_HEADS * HP), lambda i: (0, 0)),
                  pl.BlockSpec((MLA_W, KV_RANK), lambda i: (0, 0)),
                  pl.BlockSpec((1, HP), lambda i: (0, 0))],
        out_specs=[pl.BlockSpec((tm, MLA_HEADS * HP), lambda i: (i, 0)),
                   pl.BlockSpec((None, None, MLA_W, tm), lambda i: (i // nb, i % nb, 0, 0))],
        out_shape=[jax.ShapeDtypeStruct((m, MLA_HEADS * HP), BF16),
                   jax.ShapeDtypeStruct((batch, nb, MLA_W, tm), BF16)],
        compiler_params=_cparams(("parallel",)),
        name="kv_path",
    )(ckvn, krr, wk, wvt, gk)


def _attn_kernel(q_ref, k_ref, vt_ref, o_ref):
    t = ATTN_T
    qi = pl.program_id(2)
    row = lax.broadcasted_iota(jnp.int32, (t, t), 0)
    col = lax.broadcasted_iota(jnp.int32, (t, t), 1)
    causal = row <= col

    def scores(j, hh):
        k = k_ref[pl.ds(pl.multiple_of(j * t, t), t), hh * HP:(hh + 1) * HP]
        return _dot_nt(k, q_ref[:, hh * HP:(hh + 1) * HP])

    def vt(j, hh):
        return vt_ref[j, hh * V_D:(hh + 1) * V_D, :]

    init = []
    for hh in range(2):
        st = jnp.where(causal, scores(qi, hh), -jnp.inf)
        m = jnp.max(st, axis=0, keepdims=True)
        p = jnp.exp2(st - m)
        init += [m, jnp.sum(p, axis=0, keepdims=True), _dot(vt(qi, hh), p.astype(BF16))]

    def body(j, carry):
        out = []
        for hh in range(2):
            m, l, acc = carry[3 * hh:3 * hh + 3]
            st = scores(j, hh)
            m_new = jnp.maximum(m, jnp.max(st, axis=0, keepdims=True))
            p = jnp.exp2(st - m_new)
            alpha = jnp.exp2(m - m_new)
            out += [m_new, alpha * l + jnp.sum(p, axis=0, keepdims=True),
                    alpha * acc + _dot(vt(j, hh), p.astype(BF16))]
        return tuple(out)

    fin = lax.fori_loop(0, qi, body, tuple(init))
    o_t = jnp.concatenate([fin[2] / fin[1], fin[5] / fin[4]], axis=0)
    o_ref[...] = o_t.T


def prompt_attention(q, k, vt, batch, seq):
    t = ATTN_T
    nb = seq // t
    return pl.pallas_call(
        _attn_kernel,
        grid=(batch, MLA_HEADS // 2, nb),
        in_specs=[pl.BlockSpec((None, t, 2 * HP), lambda b, p, i: (b, i, p)),
                  pl.BlockSpec((None, seq, 2 * HP), lambda b, p, i: (b, 0, p)),
                  pl.BlockSpec((None, nb, 2 * V_D, t), lambda b, p, i: (b, 0, p, 0))],
        out_specs=pl.BlockSpec((None, t, 2 * V_D), lambda b, p, i: (b, i, p)),
        out_shape=jax.ShapeDtypeStruct((batch, seq, MLA_W), F32),
        compiler_params=_cparams(("parallel", "parallel", "arbitrary")),
        name="prompt_attention",
    )(q, k, vt)


def _log_sigmoid(x):
    return jnp.minimum(x, 0.0) - jnp.log1p(jnp.exp(-jnp.abs(x)))


def _mlstm_kernel(q_ref, k_ref, v_ref, g_ref, bias_ref, tri_ref,
                  h_ref, c_out, n_out, m_out, c_s, n_s, m_s):
    L = ML_L
    ci = pl.program_id(1)

    @pl.when(ci == 0)
    def _():
        c_s[...] = jnp.zeros_like(c_s)
        n_s[...] = jnp.zeros_like(n_s)
        m_s[...] = jnp.zeros_like(m_s)

    gb = g_ref[...] + bias_ref[...]
    lf = _log_sigmoid(gb)
    hi, mid, lo = _split3(lf)
    tri = tri_ref[...]
    b_all = _dot(tri, hi) + _dot(tri, mid) + _dot(tri, lo)
    b_t = b_all.T
    gb_t = gb.T
    q = q_ref[...]
    k = k_ref[...] * (ML_D ** -0.5)
    v = v_ref[...]
    row = lax.broadcasted_iota(jnp.int32, (L, L), 0)
    col = lax.broadcasted_iota(jnp.int32, (L, L), 1)
    causal = col <= row

    w_end_cols = []
    for h in range(ML_HEADS):
        b_col = b_all[:, GATE_F + h:GATE_F + h + 1]
        li_col = gb[:, GATE_I + h:GATE_I + h + 1]
        b_row = b_t[GATE_F + h:GATE_F + h + 1, :]
        li_row = gb_t[GATE_I + h:GATE_I + h + 1, :]
        m0 = m_s[h]
        c0 = c_s[h]
        n0 = n_s[h]
        qh = q[:, h * ML_D:(h + 1) * ML_D]
        kh = k[:, h * ML_D:(h + 1) * ML_D]
        vh = v[:, h * ML_D:(h + 1) * ML_D]
        qb, kb = qh.astype(BF16), kh.astype(BF16)
        a_col = b_col + m0
        dmat = jnp.where(causal, b_col - b_row + li_row, -jnp.inf)
        m_col = jnp.maximum(a_col, jnp.max(dmat, axis=1, keepdims=True))
        w = jnp.exp(dmat - m_col)
        ws = jnp.exp(a_col - m_col)
        s = _dot_nt(qb, kb) * w
        num = _dot(s.astype(BF16), vh.astype(BF16)) + _dot_nt(qb, c0.astype(BF16)) * ws
        den = jnp.sum(s, axis=1, keepdims=True) + ws * jnp.sum(qh * n0, axis=1, keepdims=True)
        h_ref[:, h * ML_D:(h + 1) * ML_D] = num / jnp.maximum(jnp.abs(den), jnp.exp(-m_col))
        m_end = m_col[L - 1:L, :]
        b_end = b_col[L - 1:L, :]
        w_end = jnp.exp(b_end - b_col + li_col - m_end)
        decay = jnp.exp(b_end + m0 - m_end)
        w_end_cols.append(w_end)
        n_s[h] = decay * n0 + jnp.sum(w_end * kh, axis=0, keepdims=True)
        m_s[h] = m_end
        c_s[h] = decay * c0

    vw = jnp.concatenate([v[:, h * ML_D:(h + 1) * ML_D] * w_end_cols[h] for h in range(ML_HEADS)], axis=1)
    vw_t = vw.T.astype(BF16)
    kb_all = k.astype(BF16)
    for h in range(ML_HEADS):
        c_s[h] = c_s[h] + _dot(vw_t[h * ML_D:(h + 1) * ML_D, :], kb_all[:, h * ML_D:(h + 1) * ML_D])

    @pl.when(ci == pl.num_programs(1) - 1)
    def _():
        c_out[...] = c_s[...]
        n_out[...] = n_s[...]
        m_out[...] = m_s[...]


def mlstm_prompt(h, bias, tri, batch, seq):
    L = ML_L
    nc = seq // L
    qb, kb, vb = (3 * CONV_W + Q_RANK + KV_RANK) // ML_W + np.arange(3)
    gate_blk = IN_MAIN // LANE

    def col(cb):
        return pl.BlockSpec((L, ML_W), lambda b, c: (b * nc + c, cb))

    return pl.pallas_call(
        _mlstm_kernel,
        grid=(batch, nc),
        in_specs=[col(int(qb)), col(int(kb)), col(int(vb)),
                  pl.BlockSpec((L, LANE), lambda b, c: (b * nc + c, gate_blk)),
                  pl.BlockSpec((1, LANE), lambda b, c: (0, 0)),
                  pl.BlockSpec((L, L), lambda b, c: (0, 0))],
        out_specs=[pl.BlockSpec((L, ML_W), lambda b, c: (b * nc + c, 0)),
                   pl.BlockSpec((None, ML_HEADS, ML_D, ML_D), lambda b, c: (b, 0, 0, 0)),
                   pl.BlockSpec((None, ML_HEADS, 1, ML_D), lambda b, c: (b, 0, 0, 0)),
                   pl.BlockSpec((None, ML_HEADS, 1, 1), lambda b, c: (b, 0, 0, 0))],
        out_shape=[jax.ShapeDtypeStruct((batch * seq, ML_W), F32),
                   jax.ShapeDtypeStruct((batch, ML_HEADS, ML_D, ML_D), F32),
                   jax.ShapeDtypeStruct((batch, ML_HEADS, 1, ML_D), F32),
                   jax.ShapeDtypeStruct((batch, ML_HEADS, 1, 1), F32)],
        scratch_shapes=[pltpu.VMEM((ML_HEADS, ML_D, ML_D), F32),
                        pltpu.VMEM((ML_HEADS, 1, ML_D), F32),
                        pltpu.VMEM((ML_HEADS, 1, 1), F32)],
        compiler_params=_cparams(("parallel", "arbitrary")),
        name="mlstm_prompt",
    )(h, h, h, h, bias, tri)


def _mlstm_step_kernel(q_ref, k_ref, vc_ref, li_ref, lfp_ref, c0_ref, n0_ref, m0_ref,
                       h_ref, c1_ref, n1_ref, m1_ref):
    q = q_ref[...]
    k = k_ref[...] * (ML_D ** -0.5)
    vc = vc_ref[...]
    li = li_ref[...]
    lf = _log_sigmoid(lfp_ref[...])
    c0, n0, m0 = c0_ref[...], n0_ref[...], m0_ref[...]
    a = lf + m0
    m = jnp.maximum(a, li)
    w = jnp.exp(li - m)
    ws = jnp.exp(a - m)
    s = jnp.sum(q * k, axis=2, keepdims=True) * w
    cq = jnp.sum(c0 * q, axis=2, keepdims=True)
    num = s * vc + cq * ws
    den = s + ws * jnp.sum(n0 * q, axis=2, keepdims=True)
    h_ref[...] = num / jnp.maximum(jnp.abs(den), jnp.exp(-m))
    c1_ref[...] = ws * c0 + w * (vc * k)
    n1_ref[...] = ws * n0 + w * k
    m1_ref[...] = m


def mlstm_step(q, k, vc, li, lfp, c0, n0, m0):
    g_all = q.shape[0]
    g = 16

    def spec(shape):
        return pl.BlockSpec((g,) + shape, lambda i: (i, 0, 0))

    return pl.pallas_call(
        _mlstm_step_kernel,
        grid=(g_all // g,),
        in_specs=[spec((1, ML_D)), spec((1, ML_D)), spec((ML_D, 1)), spec((1, 1)), spec((1, 1)),
                  spec((ML_D, ML_D)), spec((1, ML_D)), spec((1, 1))],
        out_specs=[spec((ML_D, 1)), spec((ML_D, ML_D)), spec((1, ML_D)), spec((1, 1))],
        out_shape=[jax.ShapeDtypeStruct((g_all, ML_D, 1), F32),
                   jax.ShapeDtypeStruct((g_all, ML_D, ML_D), F32),
                   jax.ShapeDtypeStruct((g_all, 1, ML_D), F32),
                   jax.ShapeDtypeStruct((g_all, 1, 1), F32)],
        compiler_params=_cparams(("parallel",)),
        name="mlstm_step",
    )(q, k, vc, li, lfp, c0, n0, m0)


def _mm_res_kernel(a_ref, b_ref, r_ref, o_ref):
    o_ref[...] = r_ref[...] + _dot(a_ref[...], b_ref[...])


def matmul_residual(a, b, res, tm):
    m, kk = a.shape
    n = b.shape[1]
    return pl.pallas_call(
        _mm_res_kernel,
        grid=(m // tm,),
        in_specs=[pl.BlockSpec((tm, kk), lambda i: (i, 0)),
                  pl.BlockSpec((kk, n), lambda i: (0, 0)),
                  pl.BlockSpec((tm, n), lambda i: (i, 0))],
        out_specs=pl.BlockSpec((tm, n), lambda i: (i, 0)),
        out_shape=jax.ShapeDtypeStruct((m, n), F32),
        compiler_params=_cparams(("parallel",)),
        name="merge_matmul",
    )(a, b, res)


def _bmm_kernel(a_ref, b_ref, o_ref):
    o_ref[...] = _dot(a_ref[...], b_ref[...])


def bmm(a, b):
    nb, m, kk = a.shape
    n = b.shape[2]
    return pl.pallas_call(
        _bmm_kernel,
        grid=(nb,),
        in_specs=[pl.BlockSpec((None, m, kk), lambda i: (i, 0, 0)),
                  pl.BlockSpec((None, kk, n), lambda i: (i, 0, 0))],
        out_specs=pl.BlockSpec((None, m, n), lambda i: (i, 0, 0)),
        out_shape=jax.ShapeDtypeStruct((nb, m, n), F32),
        compiler_params=_cparams(("parallel",)),
        name="bmm",
    )(a, b)


def _decode_kernel(pt_ref, *refs):
    del pt_ref
    npg = DEC_PAGES
    c_pages = refs[:npg]
    r_pages = refs[npg:2 * npg]
    qlat_ref, qrope_ref, cnew_ref, rnew_ref, wukt_ref, o_ref, m_s, l_s, acc_s = refs[2 * npg:]
    j = pl.program_id(1)

    @pl.when(j == 0)
    def _():
        m_s[...] = jnp.full_like(m_s, -jnp.inf)
        l_s[...] = jnp.zeros_like(l_s)
        acc_s[...] = jnp.zeros_like(acc_s)

    qlat = qlat_ref[...]
    qrope = qrope_ref[...]
    wukt = wukt_ref[...]
    ones = jnp.ones((MLA_HEADS, ROPE_D), BF16)

    def update(c, r, valid):
        n = c.shape[0]
        cb = c.astype(BF16)
        knt = _dot_nt(wukt, cb)
        ssq = jnp.sum(jnp.square(knt.reshape(NOPE_D, MLA_HEADS, n)), axis=0)
        r_hi, r_lo = _split2(r * r)
        ssq = ssq + _dot_nt(ones, r_hi) + _dot_nt(ones, r_lo)
        s = (_dot_nt(qlat, cb) + _dot_nt(qrope, r.astype(BF16))) * lax.rsqrt(ssq * (1.0 / QK_D) + RMS_EPS)
        if valid is not None:
            s = jnp.where(valid, s, -jnp.inf)
        m_old = m_s[...]
        m_new = jnp.maximum(m_old, jnp.max(s, axis=1, keepdims=True))
        p = jnp.exp2(s - m_new)
        alpha = jnp.exp2(m_old - m_new)
        l_s[...] = alpha * l_s[...] + jnp.sum(p, axis=1, keepdims=True)
        acc_s[...] = alpha * acc_s[...] + _dot(p.astype(BF16), cb)
        m_s[...] = m_new

    c = jnp.concatenate([ref[...] for ref in c_pages], axis=0)
    r = jnp.concatenate([ref[...] for ref in r_pages], axis=0)
    update(c, r, None)

    @pl.when(j == pl.num_programs(1) - 1)
    def _():
        lane = lax.broadcasted_iota(jnp.int32, (MLA_HEADS, NEW_ROWS), 1)
        update(cnew_ref[...], rnew_ref[:, :ROPE_D], lane == 0)
        o_ref[...] = acc_s[...] / l_s[...]


def decode_attention(layer, page_table, cache_ckv, cache_krope, qlat, qrope, cnew, rnew, wukt):
    nseq, n_pages = page_table.shape
    npg = DEC_PAGES
    nj = n_pages // npg

    def page_spec(width, i):
        return pl.BlockSpec((None, None, PAGE_SIZE, width),
                            lambda b, j, pt: (layer, pt[b, j * npg + i], 0, 0))

    in_specs = ([page_spec(KV_RANK, i) for i in range(npg)]
                + [page_spec(ROPE_D, i) for i in range(npg)]
                + [pl.BlockSpec((None, MLA_HEADS, LANE), lambda b, j, pt: (b, 0, 0)),
                   pl.BlockSpec((None, MLA_HEADS, ROPE_D), lambda b, j, pt: (b, 0, 0)),
                   pl.BlockSpec((None, NEW_ROWS, LANE), lambda b, j, pt: (b, 0, 0)),
                   pl.BlockSpec((None, NEW_ROWS, LANE), lambda b, j, pt: (b, 0, 0)),
                   pl.BlockSpec((MLA_HEADS * NOPE_D, KV_RANK), lambda b, j, pt: (0, 0))])
    return pl.pallas_call(
        _decode_kernel,
        grid_spec=pltpu.PrefetchScalarGridSpec(
            num_scalar_prefetch=1,
            grid=(nseq, nj),
            in_specs=in_specs,
            out_specs=pl.BlockSpec((None, MLA_HEADS, KV_RANK), lambda b, j, pt: (b, 0, 0)),
            scratch_shapes=[pltpu.VMEM((MLA_HEADS, 1), F32),
                            pltpu.VMEM((MLA_HEADS, 1), F32),
                            pltpu.VMEM((MLA_HEADS, KV_RANK), F32)]),
        out_shape=jax.ShapeDtypeStruct((nseq, MLA_HEADS, KV_RANK), F32),
        compiler_params=_cparams(("parallel", "arbitrary")),
        name="decode_attention",
    )(page_table, *([cache_ckv] * npg), *([cache_krope] * npg), qlat, qrope, cnew, rnew, wukt)


def _ffn_kernel(x_ref, g_ref, w1_ref, w3_ref, w2_ref, o_ref, xn_ref):
    f = pl.program_id(1)

    @pl.when(f == 0)
    def _():
        x = x_ref[...]
        xn_ref[...] = _rms(x, g_ref[...]).astype(BF16)
        o_ref[...] = x

    xn = xn_ref[...]
    a = _dot(xn, w1_ref[...])
    b = _dot(xn, w3_ref[...])
    act = (a * jax.nn.sigmoid(a) * b).astype(BF16)
    o_ref[...] += _dot(act, w2_ref[...])


def ffn_dense(x, g, w1, w3, w2, tm, tf):
    m = x.shape[0]
    nf = w1.shape[1] // tf
    return pl.pallas_call(
        _ffn_kernel,
        grid=(m // tm, nf),
        in_specs=[pl.BlockSpec((tm, D_MODEL), lambda i, f: (i, 0)),
                  pl.BlockSpec((1, D_MODEL), lambda i, f: (0, 0)),
                  pl.BlockSpec((D_MODEL, tf), lambda i, f: (0, f)),
                  pl.BlockSpec((D_MODEL, tf), lambda i, f: (0, f)),
                  pl.BlockSpec((tf, D_MODEL), lambda i, f: (f, 0))],
        out_specs=pl.BlockSpec((tm, D_MODEL), lambda i, f: (i, 0)),
        out_shape=jax.ShapeDtypeStruct((m, D_MODEL), F32),
        scratch_shapes=[pltpu.VMEM((tm, D_MODEL), BF16)],
        compiler_params=_cparams(("parallel", "arbitrary")),
        name="ffn_dense",
    )(x, g, w1, w3, w2)


def _router_kernel(x_ref, g_ref, wh_ref, wl_ref, xn_ref, lg_ref):
    xn = _rms(x_ref[...], g_ref[...])
    hi, lo = _split2(xn)
    xn_ref[...] = hi
    lg_ref[...] = _dot(hi, wh_ref[...]) + _dot(hi, wl_ref[...]) + _dot(lo, wh_ref[...])


def router(x, g, wr_hi, wr_lo, tm):
    m = x.shape[0]
    return pl.pallas_call(
        _router_kernel,
        grid=(m // tm,),
        in_specs=[pl.BlockSpec((tm, D_MODEL), lambda i: (i, 0)),
                  pl.BlockSpec((1, D_MODEL), lambda i: (0, 0)),
                  pl.BlockSpec((D_MODEL, LANE), lambda i: (0, 0)),
                  pl.BlockSpec((D_MODEL, LANE), lambda i: (0, 0))],
        out_specs=[pl.BlockSpec((tm, D_MODEL), lambda i: (i, 0)),
                   pl.BlockSpec((tm, LANE), lambda i: (i, 0))],
        out_shape=[jax.ShapeDtypeStruct((m, D_MODEL), BF16),
                   jax.ShapeDtypeStruct((m, LANE), F32)],
        compiler_params=_cparams(("parallel",)),
        name="router",
    )(x, g, wr_hi, wr_lo)


def _moe_block_kernel(be_ref, nu_ref, x_ref, w1_ref, w3_ref, w2_ref, o_ref):
    del be_ref
    i = pl.program_id(0)

    @pl.when(i < nu_ref[0])
    def _():
        x = x_ref[...]
        a = _dot(x, w1_ref[...])
        b = _dot(x, w3_ref[...])
        act = (a * jax.nn.sigmoid(a) * b).astype(BF16)
        o_ref[...] = _dot(act, w2_ref[...])

    @pl.when(i >= nu_ref[0])
    def _():
        o_ref[...] = jnp.zeros_like(o_ref)


def moe_blocks(blk_exp, n_used, xg, w1, w3, w2):
    n_slot = xg.shape[0]
    tm = MOE_TM
    return pl.pallas_call(
        _moe_block_kernel,
        grid_spec=pltpu.PrefetchScalarGridSpec(
            num_scalar_prefetch=2,
            grid=(n_slot // tm,),
            in_specs=[pl.BlockSpec((tm, D_MODEL), lambda i, be, nu: (i, 0)),
                      pl.BlockSpec((None, D_MODEL, D_FF_EXPERT), lambda i, be, nu: (be[i], 0, 0)),
                      pl.BlockSpec((None, D_MODEL, D_FF_EXPERT), lambda i, be, nu: (be[i], 0, 0)),
                      pl.BlockSpec((None, D_FF_EXPERT, D_MODEL), lambda i, be, nu: (be[i], 0, 0))],
            out_specs=pl.BlockSpec((tm, D_MODEL), lambda i, be, nu: (i, 0))),
        out_shape=jax.ShapeDtypeStruct((n_slot, D_MODEL), F32),
        compiler_params=_cparams(("arbitrary",)),
        name="moe_blocks",
    )(blk_exp, n_used, xg, w1, w3, w2)


def _moe_dense_kernel(x_ref, xn_ref, gate_ref, w1_ref, w3_ref, w2_ref, o_ref):
    e = pl.program_id(0)

    @pl.when(e == 0)
    def _():
        o_ref[...] = x_ref[...]

    xn = xn_ref[...]
    a = _dot(xn, w1_ref[...])
    b = _dot(xn, w3_ref[...])
    act = (a * jax.nn.sigmoid(a) * b).astype(BF16)
    o_ref[...] += _dot(act, w2_ref[...]) * gate_ref[...]


def moe_dense(x, xn, gates, w1, w3, w2):
    m = x.shape[0]
    return pl.pallas_call(
        _moe_dense_kernel,
        grid=(N_EXPERTS,),
        in_specs=[pl.BlockSpec((m, D_MODEL), lambda e: (0, 0)),
                  pl.BlockSpec((m, D_MODEL), lambda e: (0, 0)),
                  pl.BlockSpec((None, m, 1), lambda e: (e, 0, 0)),
                  pl.BlockSpec((None, D_MODEL, D_FF_EXPERT), lambda e: (e, 0, 0)),
                  pl.BlockSpec((None, D_MODEL, D_FF_EXPERT), lambda e: (e, 0, 0)),
                  pl.BlockSpec((None, D_FF_EXPERT, D_MODEL), lambda e: (e, 0, 0))],
        out_specs=pl.BlockSpec((m, D_MODEL), lambda e: (0, 0)),
        out_shape=jax.ShapeDtypeStruct((m, D_MODEL), F32),
        compiler_params=_cparams(("arbitrary",)),
        name="moe_dense",
    )(x, xn, gates, w1, w3, w2)


def _rope_tables(pos, rope_lane0):
    inv_freq = ROPE_THETA ** (-jnp.arange(HALF_ROPE, dtype=F32) / HALF_ROPE)
    ang = pos.astype(F32)[:, None] * inv_freq[None, :]
    cos, sin = jnp.cos(ang), jnp.sin(ang)
    n = pos.shape[0]
    z = lambda w: jnp.zeros((n, w), F32)
    tail = LANE - rope_lane0 - ROPE_D
    c = jnp.concatenate([jnp.ones((n, rope_lane0), F32), cos, cos, z(tail)], axis=1)
    s1 = jnp.concatenate([z(rope_lane0), -sin, z(HALF_ROPE), z(tail)], axis=1)
    s2 = jnp.concatenate([z(rope_lane0), z(HALF_ROPE), sin, z(tail)], axis=1)
    return c, s1, s2


def _head_gain(g):
    return jnp.concatenate([g[:NOPE_D], g[NOPE_D:], g[NOPE_D:], jnp.zeros((HP - QK_D,), F32)])[None, :]


def _prep_layer(l, w_in, g_cq, w_uq, g_ckv, w_uk, w_uv, g_q, g_k, b_i, b_f, w_out):
    d = {}
    wi = w_in[l]
    kr0 = 3 * CONV_W + Q_RANK + KV_RANK
    d["w_in"] = jnp.concatenate(
        [wi[:, :kr0], wi[:, kr0 + ROPE_D:kr0 + ROPE_D + 4 * ML_W], wi[:, kr0:kr0 + ROPE_D],
         wi[:, kr0 + ROPE_D + 4 * ML_W:], jnp.zeros((D_MODEL, LANE - ROPE_D - 2 * ML_HEADS), F32)],
        axis=1).astype(BF16)
    wq = w_uq[l].reshape(Q_RANK, MLA_HEADS, QK_D)
    d["w_uq"] = jnp.pad(wq, ((0, 0), (0, 0), (0, HP - QK_D))).reshape(Q_RANK, MLA_HEADS * HP).astype(BF16)
    wk = jnp.pad(w_uk[l], ((0, 0), (0, 0), (0, HP - NOPE_D))).reshape(KV_RANK, MLA_HEADS * HP)
    place = jnp.zeros((LANE, MLA_HEADS, HP), F32)
    place = place.at[jnp.arange(ROPE_D)[:, None], jnp.arange(MLA_HEADS)[None, :],
                     NOPE_D + jnp.arange(ROPE_D)[:, None]].set(1.0)
    d["w_k"] = jnp.concatenate([wk, place.reshape(LANE, MLA_HEADS * HP)], axis=0).astype(BF16)
    d["w_vt"] = w_uv[l].reshape(KV_RANK, MLA_W).T.astype(BF16)
    d["w_ukt_dec"] = w_uk[l].transpose(2, 1, 0).reshape(NOPE_D * MLA_HEADS, KV_RANK).astype(BF16)
    d["w_uk_h"] = w_uk[l].transpose(1, 2, 0).astype(BF16)
    d["w_uv_h"] = w_uv[l].transpose(1, 0, 2).astype(BF16)
    d["g_cq"] = g_cq[l][None, :]
    d["g_ckv"] = g_ckv[l][None, :]
    d["g_q"] = _head_gain(g_q[l])
    d["g_k"] = _head_gain(g_k[l])
    bias = jnp.zeros((LANE,), F32).at[GATE_I:GATE_I + ML_HEADS].set(b_i[l])
    d["gate_bias"] = bias.at[GATE_F:GATE_F + ML_HEADS].set(b_f[l])[None, :]
    d["w_out"] = w_out[l].astype(BF16)
    return d


def _split_h(h):
    o = 0
    out = []
    for w in (CONV_W, CONV_W, CONV_W, Q_RANK, KV_RANK, ML_W, ML_W, ML_W, ML_W):
        out.append(h[:, o:o + w])
        o += w
    return out


def _ml_merge(h_ml, mo, g_ml_l):
    hm = h_ml.reshape(-1, ML_HEADS, ML_D)
    ml = hm * lax.rsqrt(jnp.mean(hm * hm, axis=-1, keepdims=True) + RMS_EPS) * g_ml_l[None]
    return ml.reshape(-1, ML_W) * jax.nn.sigmoid(mo)


def _route(logits):
    top_v, top_i = lax.top_k(logits[:, :N_EXPERTS], TOP_K)
    return jax.nn.softmax(top_v, axis=-1), top_i


def kernel(x_prompt, x_sample, cache_ckv, cache_krope, page_table, state_conv, state_C, state_n, state_m,
           g_norm_mix, g_norm_ffn, w_in, conv_w, g_cq, w_uq, g_ckv, w_uk, w_uv, g_q, g_k, b_i, b_f, g_ml,
           w_out, w_ff1, w_ff3, w_ff2, w_router, w_e1, w_e3, w_e2):
    batch, seq, _ = x_prompt.shape
    nseq = x_sample.shape[0]
    depth = w_in.shape[0]
    past_len = page_table.shape[1] * PAGE_SIZE
    mp = batch * seq
    tm_p = 512

    xp = x_prompt.reshape(mp, D_MODEL)
    xs = x_sample.reshape(nseq, D_MODEL)

    pos_p = jnp.arange(seq)
    pos_s = jnp.full((nseq,), past_len)
    tabs_q_p = _rope_tables(pos_p, NOPE_D)
    tabs_k_p = _rope_tables(pos_p, 0)
    tabs_q_s = _rope_tables(pos_s, NOPE_D)
    tabs_k_s = _rope_tables(pos_s, 0)
    tri = jnp.tril(jnp.ones((ML_L, ML_L), F32)).astype(BF16)
    scale_row = jnp.full((1, HP), ATTN_SCALE * LOG2E, F32)

    outs = {k: [] for k in ("ckv_p", "ckv_s", "kr_p", "kr_s", "cv_p", "cv_s",
                            "C_p", "C_s", "n_p", "n_s", "m_p", "m_s")}

    for l in range(depth):
        p = _prep_layer(l, w_in, g_cq, w_uq, g_ckv, w_uk, w_uv, g_q, g_k, b_i, b_f, w_out)
        gmix = g_norm_mix[l][None, :]
        gffn = g_norm_ffn[l][None, :]
        cw = conv_w[l]

        h = inproj(xp, gmix, p["w_in"], tm_p)
        cb, cc, ch, _, _, _, _, _, mo = _split_h(h)
        q = q_path(h, p["g_cq"], p["w_uq"], p["g_q"] * scale_row, tabs_q_p, tm_p, seq // tm_p)
        ckvn, krr = ckv_path(h, p["g_ckv"], tabs_k_p, tm_p, seq // tm_p)
        k, vt = kv_path(ckvn, krr, p["w_k"], p["w_vt"], p["g_k"], batch, seq)
        o_mla = prompt_attention(q.reshape(batch, seq, -1), k.reshape(batch, seq, -1), vt, batch, seq)
        h_ml, c1, n1, m1 = mlstm_prompt(h, p["gate_bias"], tri, batch, seq)

        u = (cc * ch).reshape(batch, seq, CONV_W)
        up = jnp.pad(u, ((0, 0), (CONV_K - 1, 0), (0, 0)))
        conv_out = sum(cw[j] * up[:, j:j + seq] for j in range(CONV_K)).reshape(mp, CONV_W)
        mix = jnp.concatenate([cb * conv_out, o_mla.reshape(mp, MLA_W), _ml_merge(h_ml, mo, g_ml[l])], axis=1)
        xp = matmul_residual(mix.astype(BF16), p["w_out"], xp, tm_p)

        outs["ckv_p"].append(ckvn.reshape(batch, seq, KV_RANK))
        outs["kr_p"].append(krr[:, :ROPE_D].reshape(batch, seq, ROPE_D))
        outs["cv_p"].append(u[:, seq - (CONV_K - 1):])
        outs["C_p"].append(c1)
        outs["n_p"].append(n1.reshape(batch, ML_HEADS, ML_D))
        outs["m_p"].append(m1.reshape(batch, ML_HEADS))

        hs = inproj(xs, gmix, p["w_in"], nseq)
        cb, cc, ch, _, _, mq, mk, mv, mo = _split_h(hs)
        gates = hs[:, IN_MAIN:]
        qs = q_path(hs, p["g_cq"], p["w_uq"], p["g_q"] * p["g_k"] * scale_row, tabs_q_s, nseq, 1)
        ckvn_s, krr_s = ckv_path(hs, p["g_ckv"], tabs_k_s, nseq, 1)
        qs3 = qs.reshape(nseq, MLA_HEADS, HP)
        qlat = bmm(qs3[:, :, :NOPE_D].transpose(1, 0, 2), p["w_uk_h"])
        qlat = qlat.transpose(1, 0, 2).astype(BF16)
        qrope = qs3[:, :, NOPE_D:QK_D]
        cnew = jnp.pad(ckvn_s[:, None, :], ((0, 0), (0, NEW_ROWS - 1), (0, 0)))
        rnew = jnp.pad(krr_s[:, None, :], ((0, 0), (0, NEW_ROWS - 1), (0, 0)))
        pc = decode_attention(l, page_table, cache_ckv, cache_krope, qlat, qrope, cnew, rnew, p["w_ukt_dec"])
        o_s = bmm(pc.transpose(1, 0, 2).astype(BF16), p["w_uv_h"])
        o_s = o_s.transpose(1, 0, 2).reshape(nseq, MLA_W)

        g_all = nseq * ML_HEADS
        li = (gates[:, GATE_I:GATE_I + ML_HEADS] + b_i[l][None, :]).reshape(g_all, 1, 1)
        lfp = (gates[:, GATE_F:GATE_F + ML_HEADS] + b_f[l][None, :]).reshape(g_all, 1, 1)
        h_col, c1s, n1s, m1s = mlstm_step(
            mq.reshape(g_all, 1, ML_D), mk.reshape(g_all, 1, ML_D), mv.reshape(g_all, ML_D, 1), li, lfp,
            state_C[l].reshape(g_all, ML_D, ML_D), state_n[l].reshape(g_all, 1, ML_D),
            state_m[l].reshape(g_all, 1, 1))

        us = cc * ch
        st = state_conv[l]
        conv_s = cw[0] * st[:, 0] + cw[1] * st[:, 1] + cw[2] * us
        mix = jnp.concatenate([cb * conv_s, o_s, _ml_merge(h_col.reshape(nseq, ML_W), mo, g_ml[l])], axis=1)
        xs = matmul_residual(mix.astype(BF16), p["w_out"], xs, nseq)

        outs["ckv_s"].append(ckvn_s[:, None, :])
        outs["kr_s"].append(krr_s[:, None, :ROPE_D])
        outs["cv_s"].append(jnp.stack([st[:, 1], us], axis=1))
        outs["C_s"].append(c1s.reshape(nseq, ML_HEADS, ML_D, ML_D))
        outs["n_s"].append(n1s.reshape(nseq, ML_HEADS, ML_D))
        outs["m_s"].append(m1s.reshape(nseq, ML_HEADS))

        j = l // 2
        if l % 2 == 0:
            w1, w3, w2 = w_ff1[j].astype(BF16), w_ff3[j].astype(BF16), w_ff2[j].astype(BF16)
            xp = ffn_dense(xp, gffn, w1, w3, w2, tm_p, D_FF // 2)
            xs = ffn_dense(xs, gffn, w1, w3, w2, nseq, D_FF // 2)
        else:
            w1, w3, w2 = w_e1[j].astype(BF16), w_e3[j].astype(BF16), w_e2[j].astype(BF16)
            wr = jnp.pad(w_router[j], ((0, 0), (0, LANE - N_EXPERTS)))
            wr_hi = wr.astype(BF16)
            wr_lo = (wr - wr_hi.astype(F32)).astype(BF16)

            xn, logits = router(xp, gffn, wr_hi, wr_lo, tm_p)
            gate, top_i = _route(logits)
            e_flat = top_i.reshape(-1)
            n_assign = mp * TOP_K
            onehot = (e_flat[:, None] == jnp.arange(N_EXPERTS)[None, :]).astype(jnp.int32)
            rank = jnp.take_along_axis(jnp.cumsum(onehot, axis=0) - onehot, e_flat[:, None], axis=1)[:, 0]
            counts = jnp.sum(onehot, axis=0)
            padded = (counts + MOE_TM - 1) // MOE_TM * MOE_TM
            pad_end = jnp.cumsum(padded)
            pad_start = pad_end - padded
            dest = pad_start[e_flat] + rank
            n_blk = (n_assign + N_EXPERTS * (MOE_TM - 1) + MOE_TM - 1) // MOE_TM
            n_slot = n_blk * MOE_TM
            slot_tok = jnp.zeros((n_slot,), jnp.int32).at[dest].set(jnp.arange(n_assign, dtype=jnp.int32) // TOP_K)
            blk_exp = jnp.minimum(jnp.searchsorted(pad_end, jnp.arange(n_blk) * MOE_TM, side="right"),
                                  N_EXPERTS - 1).astype(jnp.int32)
            n_used = (pad_end[-1:] // MOE_TM).astype(jnp.int32)
            yb = moe_blocks(blk_exp, n_used, xn[slot_tok], w1, w3, w2)
            yg = yb[dest].reshape(mp, TOP_K, D_MODEL)
            xp = xp + jnp.sum(yg * gate[:, :, None], axis=1)

            xn_s, logits_s = router(xs, gffn, wr_hi, wr_lo, nseq)
            gate_s, top_s = _route(logits_s)
            dense_gate = jnp.sum(
                (top_s[:, :, None] == jnp.arange(N_EXPERTS)[None, None, :]) * gate_s[:, :, None], axis=1)
            xs = moe_dense(xs, xn_s, dense_gate.T[:, :, None], w1, w3, w2)

    return (xp.reshape(batch, seq, D_MODEL), xs.reshape(nseq, 1, D_MODEL),
            jnp.stack(outs["ckv_p"]), jnp.stack(outs["ckv_s"]),
            jnp.stack(outs["kr_p"]), jnp.stack(outs["kr_s"]),
            jnp.stack(outs["cv_p"]), jnp.stack(outs["cv_s"]),
            jnp.stack(outs["C_p"]), jnp.stack(outs["C_s"]),
            jnp.stack(outs["n_p"]), jnp.stack(outs["n_s"]),
            jnp.stack(outs["m_p"]), jnp.stack(outs["m_s"]))
```

```python
import functools
import math

import jax
import jax.numpy as jnp
import numpy as np
from jax import lax
from jax.experimental import pallas as pl
from jax.experimental.pallas import tpu as pltpu

F32 = jnp.float32
BF16 = jnp.bfloat16

D_MODEL = 1024
PAGE_SIZE = 128
CONV_W = 256
CONV_K = 3
MLA_HEADS = 8
NOPE_D = 64
ROPE_D = 32
HALF_ROPE = ROPE_D // 2
QK_D = NOPE_D + ROPE_D
V_D = 64
MLA_W = MLA_HEADS * V_D
Q_RANK = 384
KV_RANK = 128
ROPE_THETA = 10000.0
ATTN_SCALE = QK_D ** -0.5
ML_HEADS = 4
ML_D = 64
ML_W = ML_HEADS * ML_D
D_FF = 2816
N_EXPERTS = 8
TOP_K = 2
D_FF_EXPERT = 1408
RMS_EPS = 1e-6
LOG2E = math.log2(math.e)

LANE = 128
HP = LANE
IN_MAIN = 2304
IN_PAD = IN_MAIN + LANE
GATE_I = ROPE_D
GATE_F = ROPE_D + ML_HEADS
VMEM_LIMIT = 56 * 1024 * 1024

ATTN_T = 512
ATTN_HG = 2
ATTN_SUM_ROWS = 16
ML_L = 128
DEC_PAGES = 16
DEC_CHUNK = 4
Q_ROWS = 16
MOE_TM = 512


def _cparams(sem):
    return pltpu.CompilerParams(dimension_semantics=sem, vmem_limit_bytes=VMEM_LIMIT)


def _rms(x, g):
    return x * lax.rsqrt(jnp.mean(x * x, axis=-1, keepdims=True) + RMS_EPS) * g


def _dot(a, b):
    return jnp.dot(a, b, preferred_element_type=F32)


def _dot_nt(a, b):
    return lax.dot_general(a, b, (((1,), (1,)), ((), ())), preferred_element_type=F32)


def _split2(x):
    hi = x.astype(BF16)
    lo = (x - hi.astype(F32)).astype(BF16)
    return hi, lo


def _split3(x):
    hi = x.astype(BF16)
    r = x - hi.astype(F32)
    mid = r.astype(BF16)
    lo = (r - mid.astype(F32)).astype(BF16)
    return hi, mid, lo


def _inproj_kernel(x_ref, g_ref, w_ref, o_ref):
    xn = _rms(x_ref[...], g_ref[...]).astype(BF16)
    o_ref[...] = _dot(xn, w_ref[...])


def inproj(x, g, w, tm):
    m = x.shape[0]
    n = w.shape[1]
    return pl.pallas_call(
        _inproj_kernel,
        grid=(m // tm,),
        in_specs=[pl.BlockSpec((tm, D_MODEL), lambda i: (i, 0)),
                  pl.BlockSpec((1, D_MODEL), lambda i: (0, 0)),
                  pl.BlockSpec((D_MODEL, n), lambda i: (0, 0))],
        out_specs=pl.BlockSpec((tm, n), lambda i: (i, 0)),
        out_shape=jax.ShapeDtypeStruct((m, n), F32),
        compiler_params=_cparams(("parallel",)),
        name="inproj",
    )(x, g, w)


def _q_kernel(cq_ref, gcq_ref, wuq_ref, gq_ref, c_ref, s1_ref, s2_ref, o_ref):
    cqn = _rms(cq_ref[...], gcq_ref[...]).astype(BF16)
    q = _dot(cqn, wuq_ref[...])
    c, s1, s2, gq = c_ref[...], s1_ref[...], s2_ref[...], gq_ref[...]
    for h in range(MLA_HEADS):
        qh = q[:, h * HP:(h + 1) * HP]
        qh = qh * c + pltpu.roll(qh, HP - HALF_ROPE, 1) * s1 + pltpu.roll(qh, HALF_ROPE, 1) * s2
        ssq = jnp.sum(qh * qh, axis=-1, keepdims=True)
        qh = qh * lax.rsqrt(ssq * (1.0 / QK_D) + RMS_EPS) * gq
        o_ref[:, h * HP:(h + 1) * HP] = qh.astype(o_ref.dtype)


def q_path(h, gcq, wuq, gq, tabs, tm, n_pos_blk):
    m = h.shape[0]
    cq_blk = (3 * CONV_W) // Q_RANK
    tab_spec = pl.BlockSpec((tm, HP), lambda i: (i % n_pos_blk, 0))
    return pl.pallas_call(
        _q_kernel,
        grid=(m // tm,),
        in_specs=[pl.BlockSpec((tm, Q_RANK), lambda i: (i, cq_blk)),
                  pl.BlockSpec((1, Q_RANK), lambda i: (0, 0)),
                  pl.BlockSpec((Q_RANK, MLA_HEADS * HP), lambda i: (0, 0)),
                  pl.BlockSpec((1, HP), lambda i: (0, 0)),
                  tab_spec, tab_spec, tab_spec],
        out_specs=pl.BlockSpec((tm, MLA_HEADS * HP), lambda i: (i, 0)),
        out_shape=jax.ShapeDtypeStruct((m, MLA_HEADS * HP), BF16),
        compiler_params=_cparams(("parallel",)),
        name="q_path",
    )(h, gcq, wuq, gq, *tabs)


def _ckv_kernel(ckv_ref, kr_ref, g_ref, c_ref, s1_ref, s2_ref, ckvn_ref, krr_ref):
    ckvn_ref[...] = _rms(ckv_ref[...], g_ref[...])
    kr = kr_ref[...]
    krr_ref[...] = (kr * c_ref[...] + pltpu.roll(kr, LANE - HALF_ROPE, 1) * s1_ref[...]
                    + pltpu.roll(kr, HALF_ROPE, 1) * s2_ref[...])


def ckv_path(h, g_ckv, tabs, tm, n_pos_blk):
    m = h.shape[0]
    ckv_blk = (3 * CONV_W + Q_RANK) // KV_RANK
    kr_blk = IN_MAIN // LANE
    tab_spec = pl.BlockSpec((tm, LANE), lambda i: (i % n_pos_blk, 0))
    return pl.pallas_call(
        _ckv_kernel,
        grid=(m // tm,),
        in_specs=[pl.BlockSpec((tm, KV_RANK), lambda i: (i, ckv_blk)),
                  pl.BlockSpec((tm, LANE), lambda i: (i, kr_blk)),
                  pl.BlockSpec((1, KV_RANK), lambda i: (0, 0)),
                  tab_spec, tab_spec, tab_spec],
        out_specs=[pl.BlockSpec((tm, KV_RANK), lambda i: (i, 0)),
                   pl.BlockSpec((tm, LANE), lambda i: (i, 0))],
        out_shape=[jax.ShapeDtypeStruct((m, KV_RANK), F32),
                   jax.ShapeDtypeStruct((m, LANE), F32)],
        compiler_params=_cparams(("parallel",)),
        name="ckv_path",
    )(h, h, g_ckv, *tabs)


def _kv_kernel(ckvn_ref, krr_ref, wk_ref, wvt_ref, gk_ref, k_ref, vt_ref):
    cb = ckvn_ref[...].astype(BF16)
    ck = jnp.concatenate([cb, krr_ref[...].astype(BF16)], axis=1)
    k = _dot(ck, wk_ref[...])
    gk = gk_ref[...]
    for h in range(MLA_HEADS):
        kh = k[:, h * HP:(h + 1) * HP]
        ssq = jnp.sum(kh * kh, axis=-1, keepdims=True)
        k_ref[:, h * HP:(h + 1) * HP] = (kh * lax.rsqrt(ssq * (1.0 / QK_D) + RMS_EPS) * gk).astype(BF16)
    vt_ref[...] = _dot_nt(wvt_ref[...], cb).astype(BF16)


def kv_path(ckvn, krr, wk, wvt, gk, batch, seq):
    tm = ATTN_T
    m = ckvn.shape[0]
    nb = seq // tm
    return pl.pallas_call(
        _kv_kernel,
        grid=(m // tm,),
        in_specs=[pl.BlockSpec((tm, KV_RANK), lambda i: (i, 0)),
                  pl.BlockSpec((tm, LANE), lambda i: (i, 0)),
                  pl.BlockSpec((2 * LANE, MLA_HEADS * HP), lambda i: (0, 0)),
                  pl.BlockSpec((MLA_W, KV_RANK), lambda i: (0, 0)),
                  pl.BlockSpec((1, HP), lambda i: (0, 0))],
        out_specs=[pl.BlockSpec((tm, MLA_HEADS * HP), lambda i: (i, 0)),
                   pl.BlockSpec((None, None, MLA_W, tm), lambda i: (i // nb, i % nb, 0, 0))],
        out_shape=[jax.ShapeDtypeStruct((m, MLA_HEADS * HP), BF16),
                   jax.ShapeDtypeStruct((batch, nb, MLA_W, tm), BF16)],
        compiler_params=_cparams(("parallel",)),
        name="kv_path",
    )(ckvn, krr, wk, wvt, gk)


def _attn_kernel(q_ref, k_ref, vt_ref, o_ref, s0_ref, s1_ref):
    t = ATTN_T
    hg = ATTN_HG
    qi = pl.program_id(2)
    row = lax.broadcasted_iota(jnp.int32, (t, t), 0)
    col = lax.broadcasted_iota(jnp.int32, (t, t), 1)
    causal = row <= col
    ones_rows = jnp.ones((ATTN_SUM_ROWS, t), BF16)

    def qk(j, dst, masked):
        for hh in range(hg):
            k = k_ref[pl.ds(pl.multiple_of(j * t, t), t), hh * HP:(hh + 1) * HP]
            st = _dot_nt(k, q_ref[:, hh * HP:(hh + 1) * HP])
            dst[hh] = jnp.where(causal, st, -jnp.inf) if masked else st

    def process(j, src, carry):
        out = []
        for hh in range(hg):
            m, acc = carry[2 * hh:2 * hh + 2]
            st = src[hh]
            m_new = jnp.maximum(m, jnp.max(st, axis=0, keepdims=True))
            p = jnp.exp2(st - m_new).astype(BF16)
            alpha = jnp.exp2(m - m_new)
            va = jnp.concatenate([vt_ref[j, hh * V_D:(hh + 1) * V_D, :], ones_rows], axis=0)
            out += [m_new, alpha * acc + _dot(va, p)]
        return tuple(out)

    qk(qi, s0_ref, True)
    init = (jnp.full((1, t), -jnp.inf, F32), jnp.zeros((V_D + ATTN_SUM_ROWS, t), F32)) * hg

    def body(tt, carry):
        qk(2 * tt, s1_ref, False)
        carry = process(jnp.where(tt == 0, qi, 2 * tt - 1), s0_ref, carry)
        qk(jnp.minimum(2 * tt + 1, jnp.maximum(qi - 1, 0)), s0_ref, False)
        return process(2 * tt, s1_ref, carry)

    carry = lax.fori_loop(0, (qi + 1) // 2, body, init)
    carry = lax.cond(qi % 2 == 0,
                     lambda c: process(jnp.where(qi == 0, 0, qi - 1), s0_ref, c),
                     lambda c: c, carry)
    o_t = jnp.concatenate([carry[2 * hh + 1][:V_D] / carry[2 * hh + 1][V_D:V_D + 1] for hh in range(hg)], axis=0)
    o_ref[...] = o_t.T


def prompt_attention(q, k, vt, batch, seq):
    t = ATTN_T
    hg = ATTN_HG
    nb = seq // t
    return pl.pallas_call(
        _attn_kernel,
        grid=(batch, MLA_HEADS // hg, nb),
        in_specs=[pl.BlockSpec((None, t, hg * HP), lambda b, p, i: (b, i, p)),
                  pl.BlockSpec((None, seq, hg * HP), lambda b, p, i: (b, 0, p)),
                  pl.BlockSpec((None, nb, hg * V_D, t), lambda b, p, i: (b, 0, p, 0))],
        out_specs=pl.BlockSpec((None, t, hg * V_D), lambda b, p, i: (b, i, p)),
        out_shape=jax.ShapeDtypeStruct((batch, seq, MLA_W), F32),
        scratch_shapes=[pltpu.VMEM((hg, t, t), F32), pltpu.VMEM((hg, t, t), F32)],
        compiler_params=_cparams(("parallel", "parallel", "arbitrary")),
        name="prompt_attention",
    )(q, k, vt)


def _log_sigmoid(x):
    return jnp.minimum(x, 0.0) - jnp.log1p(jnp.exp(-jnp.abs(x)))


def _mlstm_kernel(q_ref, k_ref, v_ref, g_ref, bias_ref, tri_ref,
                  h_ref, c_out, n_out, m_out, c_s, n_s, m_s):
    L = ML_L
    ci = pl.program_id(1)

    @pl.when(ci == 0)
    def _():
        c_s[...] = jnp.zeros_like(c_s)
        n_s[...] = jnp.zeros_like(n_s)
        m_s[...] = jnp.zeros_like(m_s)

    gb = g_ref[...] + bias_ref[...]
    lf = _log_sigmoid(gb)
    hi, mid, lo = _split3(lf)
    tri = tri_ref[...]
    b_all = _dot(tri, hi) + _dot(tri, mid) + _dot(tri, lo)
    b_t = b_all.T
    gb_t = gb.T
    q = q_ref[...]
    k = k_ref[...] * (ML_D ** -0.5)
    v = v_ref[...]
    row = lax.broadcasted_iota(jnp.int32, (L, L), 0)
    col = lax.broadcasted_iota(jnp.int32, (L, L), 1)
    causal = col <= row

    w_end_cols = []
    for h in range(ML_HEADS):
        b_col = b_all[:, GATE_F + h:GATE_F + h + 1]
        li_col = gb[:, GATE_I + h:GATE_I + h + 1]
        b_row = b_t[GATE_F + h:GATE_F + h + 1, :]
        li_row = gb_t[GATE_I + h:GATE_I + h + 1, :]
        m0 = m_s[h]
        c0 = c_s[h]
        n0 = n_s[h]
        qh = q[:, h * ML_D:(h + 1) * ML_D]
        kh = k[:, h * ML_D:(h + 1) * ML_D]
        vh = v[:, h * ML_D:(h + 1) * ML_D]
        qb, kb = qh.astype(BF16), kh.astype(BF16)
        a_col = b_col + m0
        dmat = jnp.where(causal, b_col - b_row + li_row, -jnp.inf)
        m_col = jnp.maximum(a_col, jnp.max(dmat, axis=1, keepdims=True))
        w = jnp.exp(dmat - m_col)
        ws = jnp.exp(a_col - m_col)
        s = _dot_nt(qb, kb) * w
        num = _dot(s.astype(BF16), vh.astype(BF16)) + _dot_nt(qb, c0.astype(BF16)) * ws
        den = jnp.sum(s, axis=1, keepdims=True) + ws * jnp.sum(qh * n0, axis=1, keepdims=True)
        h_ref[:, h * ML_D:(h + 1) * ML_D] = num / jnp.maximum(jnp.abs(den), jnp.exp(-m_col))
        m_end = m_col[L - 1:L, :]
        b_end = b_col[L - 1:L, :]
        w_end = jnp.exp(b_end - b_col + li_col - m_end)
        decay = jnp.exp(b_end + m0 - m_end)
        w_end_cols.append(w_end)
        n_s[h] = decay * n0 + jnp.sum(w_end * kh, axis=0, keepdims=True)
        m_s[h] = m_end
        c_s[h] = decay * c0

    vw = jnp.concatenate([v[:, h * ML_D:(h + 1) * ML_D] * w_end_cols[h] for h in range(ML_HEADS)], axis=1)
    vw_t = vw.T.astype(BF16)
    kb_all = k.astype(BF16)
    for h in range(ML_HEADS):
        c_s[h] = c_s[h] + _dot(vw_t[h * ML_D:(h + 1) * ML_D, :], kb_all[:, h * ML_D:(h + 1) * ML_D])

    @pl.when(ci == pl.num_programs(1) - 1)
    def _():
        c_out[...] = c_s[...]
        n_out[...] = n_s[...]
        m_out[...] = m_s[...]


def mlstm_prompt(h, bias, tri, batch, seq):
    L = ML_L
    nc = seq // L
    qb, kb, vb = (3 * CONV_W + Q_RANK + KV_RANK) // ML_W + np.arange(3)
    gate_blk = IN_MAIN // LANE

    def col(cb):
        return pl.BlockSpec((L, ML_W), lambda b, c: (b * nc + c, cb))

    return pl.pallas_call(
        _mlstm_kernel,
        grid=(batch, nc),
        in_specs=[col(int(qb)), col(int(kb)), col(int(vb)),
                  pl.BlockSpec((L, LANE), lambda b, c: (b * nc + c, gate_blk)),
                  pl.BlockSpec((1, LANE), lambda b, c: (0, 0)),
                  pl.BlockSpec((L, L), lambda b, c: (0, 0))],
        out_specs=[pl.BlockSpec((L, ML_W), lambda b, c: (b * nc + c, 0)),
                   pl.BlockSpec((None, ML_HEADS, ML_D, ML_D), lambda b, c: (b, 0, 0, 0)),
                   pl.BlockSpec((None, ML_HEADS, 1, ML_D), lambda b, c: (b, 0, 0, 0)),
                   pl.BlockSpec((None, ML_HEADS, 1, 1), lambda b, c: (b, 0, 0, 0))],
        out_shape=[jax.ShapeDtypeStruct((batch * seq, ML_W), F32),
                   jax.ShapeDtypeStruct((batch, ML_HEADS, ML_D, ML_D), F32),
                   jax.ShapeDtypeStruct((batch, ML_HEADS, 1, ML_D), F32),
                   jax.ShapeDtypeStruct((batch, ML_HEADS, 1, 1), F32)],
        scratch_shapes=[pltpu.VMEM((ML_HEADS, ML_D, ML_D), F32),
                        pltpu.VMEM((ML_HEADS, 1, ML_D), F32),
                        pltpu.VMEM((ML_HEADS, 1, 1), F32)],
        compiler_params=_cparams(("parallel", "arbitrary")),
        name="mlstm_prompt",
    )(h, h, h, h, bias, tri)


def _mlstm_step_kernel(q_ref, k_ref, vc_ref, li_ref, lfp_ref, c0_ref, n0_ref, m0_ref,
                       h_ref, c1_ref, n1_ref, m1_ref):
    q = q_ref[...]
    k = k_ref[...] * (ML_D ** -0.5)
    vc = vc_ref[...]
    li = li_ref[...]
    lf = _log_sigmoid(lfp_ref[...])
    c0, n0, m0 = c0_ref[...], n0_ref[...], m0_ref[...]
    a = lf + m0
    m = jnp.maximum(a, li)
    w = jnp.exp(li - m)
    ws = jnp.exp(a - m)
    s = jnp.sum(q * k, axis=2, keepdims=True) * w
    cq = jnp.sum(c0 * q, axis=2, keepdims=True)
    num = s * vc + cq * ws
    den = s + ws * jnp.sum(n0 * q, axis=2, keepdims=True)
    h_ref[...] = num / jnp.maximum(jnp.abs(den), jnp.exp(-m))
    c1_ref[...] = ws * c0 + w * (vc * k)
    n1_ref[...] = ws * n0 + w * k
    m1_ref[...] = m


def mlstm_step(q, k, vc, li, lfp, c0, n0, m0):
    g_all = q.shape[0]
    g = 16

    def spec(shape):
        return pl.BlockSpec((g,) + shape, lambda i: (i, 0, 0))

    return pl.pallas_call(
        _mlstm_step_kernel,
        grid=(g_all // g,),
        in_specs=[spec((1, ML_D)), spec((1, ML_D)), spec((ML_D, 1)), spec((1, 1)), spec((1, 1)),
                  spec((ML_D, ML_D)), spec((1, ML_D)), spec((1, 1))],
        out_specs=[spec((ML_D, 1)), spec((ML_D, ML_D)), spec((1, ML_D)), spec((1, 1))],
        out_shape=[jax.ShapeDtypeStruct((g_all, ML_D, 1), F32),
                   jax.ShapeDtypeStruct((g_all, ML_D, ML_D), F32),
                   jax.ShapeDtypeStruct((g_all, 1, ML_D), F32),
                   jax.ShapeDtypeStruct((g_all, 1, 1), F32)],
        compiler_params=_cparams(("parallel",)),
        name="mlstm_step",
    )(q, k, vc, li, lfp, c0, n0, m0)


def _mm_res_kernel(a_ref, b_ref, r_ref, o_ref):
    o_ref[...] = r_ref[...] + _dot(a_ref[...], b_ref[...])


def matmul_residual(a, b, res, tm):
    m, kk = a.shape
    n = b.shape[1]
    return pl.pallas_call(
        _mm_res_kernel,
        grid=(m // tm,),
        in_specs=[pl.BlockSpec((tm, kk), lambda i: (i, 0)),
                  pl.BlockSpec((kk, n), lambda i: (0, 0)),
                  pl.BlockSpec((tm, n), lambda i: (i, 0))],
        out_specs=pl.BlockSpec((tm, n), lambda i: (i, 0)),
        out_shape=jax.ShapeDtypeStruct((m, n), F32),
        compiler_params=_cparams(("parallel",)),
        name="merge_matmul",
    )(a, b, res)


def _bmm_kernel(a_ref, b_ref, o_ref):
    o_ref[...] = _dot(a_ref[...], b_ref[...])


def bmm(a, b):
    nb, m, kk = a.shape
    n = b.shape[2]
    return pl.pallas_call(
        _bmm_kernel,
        grid=(nb,),
        in_specs=[pl.BlockSpec((None, m, kk), lambda i: (i, 0, 0)),
                  pl.BlockSpec((None, kk, n), lambda i: (i, 0, 0))],
        out_specs=pl.BlockSpec((None, m, n), lambda i: (i, 0, 0)),
        out_shape=jax.ShapeDtypeStruct((nb, m, n), F32),
        compiler_params=_cparams(("parallel",)),
        name="bmm",
    )(a, b)


def _decode_kernel(pt_ref, *refs):
    del pt_ref
    npg = DEC_PAGES
    c_pages = refs[:npg]
    r_pages = refs[npg:2 * npg]
    qlat_ref, qrope_ref, cnew_ref, rnew_ref, wukt_ref, o_ref, m_s, l_s, acc_s = refs[2 * npg:]
    j = pl.program_id(1)

    @pl.when(j == 0)
    def _():
        m_s[...] = jnp.full_like(m_s, -jnp.inf)
        l_s[...] = jnp.zeros_like(l_s)
        acc_s[...] = jnp.zeros_like(acc_s)

    lhs = jnp.concatenate([wukt_ref[...], qlat_ref[...]], axis=0)
    qrope = qrope_ref[...]
    nk = MLA_HEADS * NOPE_D

    def products(c, rt):
        cb = c.astype(BF16)
        big = _dot_nt(lhs, cb)
        return cb, big, _dot(qrope, rt.astype(BF16)), jnp.sum(rt * rt, axis=0, keepdims=True)

    def weights(prod, valid, m_old):
        cb, big, s_rope, ssq_r = prod
        n = cb.shape[0]
        ssq = jnp.sum(jnp.square(big[:nk].reshape(NOPE_D, MLA_HEADS, n)), axis=0) + ssq_r
        s = (big[nk:nk + MLA_HEADS] + s_rope[:MLA_HEADS]) * lax.rsqrt(ssq * (1.0 / QK_D) + RMS_EPS)
        if valid is not None:
            s = jnp.where(valid, s, -jnp.inf)
        m_new = jnp.maximum(m_old, jnp.max(s, axis=1, keepdims=True))
        return m_new, jnp.exp2(m_old - m_new), jnp.exp2(s - m_new)

    def accumulate(w, cb, l_old, acc):
        _, alpha, p = w
        return alpha * l_old + jnp.sum(p, axis=1, keepdims=True), alpha * acc + _dot(p.astype(BF16), cb)

    nch = npg // DEC_CHUNK
    chunk_c = [jnp.concatenate([c_pages[i * DEC_CHUNK + k][...] for k in range(DEC_CHUNK)], axis=0)
               for i in range(nch)]
    chunk_r = [jnp.concatenate([r_pages[i * DEC_CHUNK + k][...] for k in range(DEC_CHUNK)], axis=1)
               for i in range(nch)]
    m_run, l_run, acc = m_s[...], l_s[...], acc_s[...]
    prods = [products(chunk_c[i], chunk_r[i]) for i in range(min(2, nch))]
    for i in range(nch):
        w = weights(prods[i], None, m_run)
        if i + 2 < nch:
            prods.append(products(chunk_c[i + 2], chunk_r[i + 2]))
        l_run, acc = accumulate(w, prods[i][0], l_run, acc)
        m_run = w[0]
    m_s[...], l_s[...], acc_s[...] = m_run, l_run, acc

    @pl.when(j == pl.num_programs(1) - 1)
    def _():
        lane = lax.broadcasted_iota(jnp.int32, (MLA_HEADS, PAGE_SIZE), 1)
        prod = products(cnew_ref[...], rnew_ref[...])
        l_fin, acc_fin = accumulate(weights(prod, lane == 0, m_run), prod[0], l_run, acc)
        o_ref[...] = acc_fin / l_fin


def decode_attention(layer, page_table, cache_ckv, cache_krope_t, qlat, qrope, cnew, rnew_t, wukt):
    nseq, n_pages = page_table.shape
    npg = DEC_PAGES
    nj = n_pages // npg

    def page_spec(rows, i):
        return pl.BlockSpec((None, None, rows, PAGE_SIZE),
                            lambda b, j, pt: (layer, pt[b, j * npg + i], 0, 0))

    in_specs = ([page_spec(PAGE_SIZE, i) for i in range(npg)]
                + [page_spec(ROPE_D, i) for i in range(npg)]
                + [pl.BlockSpec((None, Q_ROWS, KV_RANK), lambda b, j, pt: (b, 0, 0)),
                   pl.BlockSpec((None, Q_ROWS, ROPE_D), lambda b, j, pt: (b, 0, 0)),
                   pl.BlockSpec((None, PAGE_SIZE, KV_RANK), lambda b, j, pt: (b, 0, 0)),
                   pl.BlockSpec((None, ROPE_D, PAGE_SIZE), lambda b, j, pt: (b, 0, 0)),
                   pl.BlockSpec((MLA_HEADS * NOPE_D, KV_RANK), lambda b, j, pt: (0, 0))])
    return pl.pallas_call(
        _decode_kernel,
        grid_spec=pltpu.PrefetchScalarGridSpec(
            num_scalar_prefetch=1,
            grid=(nseq, nj),
            in_specs=in_specs,
            out_specs=pl.BlockSpec((None, MLA_HEADS, KV_RANK), lambda b, j, pt: (b, 0, 0)),
            scratch_shapes=[pltpu.VMEM((MLA_HEADS, 1), F32),
                            pltpu.VMEM((MLA_HEADS, 1), F32),
                            pltpu.VMEM((MLA_HEADS, KV_RANK), F32)]),
        out_shape=jax.ShapeDtypeStruct((nseq, MLA_HEADS, KV_RANK), F32),
        compiler_params=_cparams(("parallel", "arbitrary")),
        name="decode_attention",
    )(page_table, *([cache_ckv] * npg), *([cache_krope_t] * npg), qlat, qrope, cnew, rnew_t, wukt)


def _ffn_kernel(x_ref, g_ref, w1_ref, w3_ref, w2_ref, o_ref, xn_ref):
    f = pl.program_id(1)

    @pl.when(f == 0)
    def _():
        x = x_ref[...]
        xn_ref[...] = _rms(x, g_ref[...]).astype(BF16)
        o_ref[...] = x

    xn = xn_ref[...]
    a = _dot(xn, w1_ref[...])
    b = _dot(xn, w3_ref[...])
    act = (a * jax.nn.sigmoid(a) * b).astype(BF16)
    o_ref[...] += _dot(act, w2_ref[...])


def ffn_dense(x, g, w1, w3, w2, tm, tf):
    m = x.shape[0]
    nf = w1.shape[1] // tf
    return pl.pallas_call(
        _ffn_kernel,
        grid=(m // tm, nf),
        in_specs=[pl.BlockSpec((tm, D_MODEL), lambda i, f: (i, 0)),
                  pl.BlockSpec((1, D_MODEL), lambda i, f: (0, 0)),
                  pl.BlockSpec((D_MODEL, tf), lambda i, f: (0, f)),
                  pl.BlockSpec((D_MODEL, tf), lambda i, f: (0, f)),
                  pl.BlockSpec((tf, D_MODEL), lambda i, f: (f, 0))],
        out_specs=pl.BlockSpec((tm, D_MODEL), lambda i, f: (i, 0)),
        out_shape=jax.ShapeDtypeStruct((m, D_MODEL), F32),
        scratch_shapes=[pltpu.VMEM((tm, D_MODEL), BF16)],
        compiler_params=_cparams(("parallel", "arbitrary")),
        name="ffn_dense",
    )(x, g, w1, w3, w2)


def _router_kernel(x_ref, g_ref, wh_ref, wl_ref, xn_ref, info_ref):
    xn = _rms(x_ref[...], g_ref[...])
    hi, lo = _split2(xn)
    xn_ref[...] = hi
    lg = _dot(hi, wh_ref[...]) + _dot(hi, wl_ref[...]) + _dot(lo, wh_ref[...])
    lane = lax.broadcasted_iota(jnp.int32, lg.shape, 1).astype(F32)
    lg = jnp.where(lane < N_EXPERTS, lg, -jnp.inf)
    v1 = jnp.max(lg, axis=1, keepdims=True)
    i1 = jnp.min(jnp.where(lg == v1, lane, float(LANE)), axis=1, keepdims=True)
    rest = jnp.where(lane == i1, -jnp.inf, lg)
    v2 = jnp.max(rest, axis=1, keepdims=True)
    i2 = jnp.min(jnp.where(rest == v2, lane, float(LANE)), axis=1, keepdims=True)
    e = jnp.exp(v2 - v1)
    g1 = 1.0 / (1.0 + e)
    info_ref[...] = jnp.where(lane == 0, i1, jnp.where(lane == 1, i2, jnp.where(lane == 2, g1, e * g1)))


def router(x, g, wr_hi, wr_lo, tm):
    m = x.shape[0]
    return pl.pallas_call(
        _router_kernel,
        grid=(m // tm,),
        in_specs=[pl.BlockSpec((tm, D_MODEL), lambda i: (i, 0)),
                  pl.BlockSpec((1, D_MODEL), lambda i: (0, 0)),
                  pl.BlockSpec((D_MODEL, LANE), lambda i: (0, 0)),
                  pl.BlockSpec((D_MODEL, LANE), lambda i: (0, 0))],
        out_specs=[pl.BlockSpec((tm, D_MODEL), lambda i: (i, 0)),
                   pl.BlockSpec((tm, LANE), lambda i: (i, 0))],
        out_shape=[jax.ShapeDtypeStruct((m, D_MODEL), BF16),
                   jax.ShapeDtypeStruct((m, LANE), F32)],
        compiler_params=_cparams(("parallel",)),
        name="router",
    )(x, g, wr_hi, wr_lo)


def _moe_block_kernel(be_ref, nu_ref, x_ref, w1_ref, w3_ref, w2_ref, o_ref):
    del be_ref
    i = pl.program_id(0)

    @pl.when(i < nu_ref[0])
    def _():
        x = x_ref[...]
        a = _dot(x, w1_ref[...])
        b = _dot(x, w3_ref[...])
        act = (a * jax.nn.sigmoid(a) * b).astype(BF16)
        o_ref[...] = _dot(act, w2_ref[...])

    @pl.when(i >= nu_ref[0])
    def _():
        o_ref[...] = jnp.zeros_like(o_ref)


def moe_blocks(blk_exp, n_used, xg, w1, w3, w2):
    n_slot = xg.shape[0]
    tm = MOE_TM
    return pl.pallas_call(
        _moe_block_kernel,
        grid_spec=pltpu.PrefetchScalarGridSpec(
            num_scalar_prefetch=2,
            grid=(n_slot // tm,),
            in_specs=[pl.BlockSpec((tm, D_MODEL), lambda i, be, nu: (i, 0)),
                      pl.BlockSpec((None, D_MODEL, D_FF_EXPERT), lambda i, be, nu: (be[i], 0, 0)),
                      pl.BlockSpec((None, D_MODEL, D_FF_EXPERT), lambda i, be, nu: (be[i], 0, 0)),
                      pl.BlockSpec((None, D_FF_EXPERT, D_MODEL), lambda i, be, nu: (be[i], 0, 0))],
            out_specs=pl.BlockSpec((tm, D_MODEL), lambda i, be, nu: (i, 0))),
        out_shape=jax.ShapeDtypeStruct((n_slot, D_MODEL), F32),
        compiler_params=_cparams(("arbitrary",)),
        name="moe_blocks",
    )(blk_exp, n_used, xg, w1, w3, w2)


def _moe_dense_kernel(x_ref, xn_ref, gate_ref, w1_ref, w3_ref, w2_ref, o_ref):
    e = pl.program_id(0)

    @pl.when(e == 0)
    def _():
        o_ref[...] = x_ref[...]

    xn = xn_ref[...]
    a = _dot(xn, w1_ref[...])
    b = _dot(xn, w3_ref[...])
    act = (a * jax.nn.sigmoid(a) * b).astype(BF16)
    o_ref[...] += _dot(act, w2_ref[...]) * gate_ref[...]


def moe_dense(x, xn, gates, w1, w3, w2):
    m = x.shape[0]
    return pl.pallas_call(
        _moe_dense_kernel,
        grid=(N_EXPERTS,),
        in_specs=[pl.BlockSpec((m, D_MODEL), lambda e: (0, 0)),
                  pl.BlockSpec((m, D_MODEL), lambda e: (0, 0)),
                  pl.BlockSpec((None, m, 1), lambda e: (e, 0, 0)),
                  pl.BlockSpec((None, D_MODEL, D_FF_EXPERT), lambda e: (e, 0, 0)),
                  pl.BlockSpec((None, D_MODEL, D_FF_EXPERT), lambda e: (e, 0, 0)),
                  pl.BlockSpec((None, D_FF_EXPERT, D_MODEL), lambda e: (e, 0, 0))],
        out_specs=pl.BlockSpec((m, D_MODEL), lambda e: (0, 0)),
        out_shape=jax.ShapeDtypeStruct((m, D_MODEL), F32),
        compiler_params=_cparams(("arbitrary",)),
        name="moe_dense",
    )(x, xn, gates, w1, w3, w2)


def _rope_tables(pos, rope_lane0):
    inv_freq = ROPE_THETA ** (-jnp.arange(HALF_ROPE, dtype=F32) / HALF_ROPE)
    ang = pos.astype(F32)[:, None] * inv_freq[None, :]
    cos, sin = jnp.cos(ang), jnp.sin(ang)
    n = pos.shape[0]
    z = lambda w: jnp.zeros((n, w), F32)
    tail = LANE - rope_lane0 - ROPE_D
    c = jnp.concatenate([jnp.ones((n, rope_lane0), F32), cos, cos, z(tail)], axis=1)
    s1 = jnp.concatenate([z(rope_lane0), -sin, z(HALF_ROPE), z(tail)], axis=1)
    s2 = jnp.concatenate([z(rope_lane0), z(HALF_ROPE), sin, z(tail)], axis=1)
    return c, s1, s2


def _head_gain(g):
    return jnp.concatenate([g[:NOPE_D], g[NOPE_D:], g[NOPE_D:], jnp.zeros((HP - QK_D,), F32)])[None, :]


def _prep_layer(l, w_in, g_cq, w_uq, g_ckv, w_uk, w_uv, g_q, g_k, b_i, b_f, w_out):
    d = {}
    wi = w_in[l]
    kr0 = 3 * CONV_W + Q_RANK + KV_RANK
    d["w_in"] = jnp.concatenate(
        [wi[:, :kr0], wi[:, kr0 + ROPE_D:kr0 + ROPE_D + 4 * ML_W], wi[:, kr0:kr0 + ROPE_D],
         wi[:, kr0 + ROPE_D + 4 * ML_W:], jnp.zeros((D_MODEL, LANE - ROPE_D - 2 * ML_HEADS), F32)],
        axis=1).astype(BF16)
    wq = w_uq[l].reshape(Q_RANK, MLA_HEADS, QK_D)
    d["w_uq"] = jnp.pad(wq, ((0, 0), (0, 0), (0, HP - QK_D))).reshape(Q_RANK, MLA_HEADS * HP).astype(BF16)
    wk = jnp.pad(w_uk[l], ((0, 0), (0, 0), (0, HP - NOPE_D))).reshape(KV_RANK, MLA_HEADS * HP)
    place = jnp.zeros((LANE, MLA_HEADS, HP), F32)
    place = place.at[jnp.arange(ROPE_D)[:, None], jnp.arange(MLA_HEADS)[None, :],
                     NOPE_D + jnp.arange(ROPE_D)[:, None]].set(1.0)
    d["w_k"] = jnp.concatenate([wk, place.reshape(LANE, MLA_HEADS * HP)], axis=0).astype(BF16)
    d["w_vt"] = w_uv[l].reshape(KV_RANK, MLA_W).T.astype(BF16)
    d["w_ukt_dec"] = w_uk[l].transpose(2, 1, 0).reshape(NOPE_D * MLA_HEADS, KV_RANK).astype(BF16)
    d["w_uk_h"] = w_uk[l].transpose(1, 2, 0).astype(BF16)
    d["w_uv_h"] = w_uv[l].transpose(1, 0, 2).astype(BF16)
    d["g_cq"] = g_cq[l][None, :]
    d["g_ckv"] = g_ckv[l][None, :]
    d["g_q"] = _head_gain(g_q[l])
    d["g_k"] = _head_gain(g_k[l])
    bias = jnp.zeros((LANE,), F32).at[GATE_I:GATE_I + ML_HEADS].set(b_i[l])
    d["gate_bias"] = bias.at[GATE_F:GATE_F + ML_HEADS].set(b_f[l])[None, :]
    d["w_out"] = w_out[l].astype(BF16)
    return d


def _split_h(h):
    o = 0
    out = []
    for w in (CONV_W, CONV_W, CONV_W, Q_RANK, KV_RANK, ML_W, ML_W, ML_W, ML_W):
        out.append(h[:, o:o + w])
        o += w
    return out


def _ml_merge(h_ml, mo, g_ml_l):
    hm = h_ml.reshape(-1, ML_HEADS, ML_D)
    ml = hm * lax.rsqrt(jnp.mean(hm * hm, axis=-1, keepdims=True) + RMS_EPS) * g_ml_l[None]
    return ml.reshape(-1, ML_W) * jax.nn.sigmoid(mo)


def _route(info):
    return info[:, TOP_K:2 * TOP_K], info[:, :TOP_K].astype(jnp.int32)


def kernel(x_prompt, x_sample, cache_ckv, cache_krope, page_table, state_conv, state_C, state_n, state_m,
           g_norm_mix, g_norm_ffn, w_in, conv_w, g_cq, w_uq, g_ckv, w_uk, w_uv, g_q, g_k, b_i, b_f, g_ml,
           w_out, w_ff1, w_ff3, w_ff2, w_router, w_e1, w_e3, w_e2):
    batch, seq, _ = x_prompt.shape
    nseq = x_sample.shape[0]
    depth = w_in.shape[0]
    past_len = page_table.shape[1] * PAGE_SIZE
    mp = batch * seq
    tm_p = 512

    xp = x_prompt.reshape(mp, D_MODEL)
    xs = x_sample.reshape(nseq, D_MODEL)

    pos_p = jnp.arange(seq)
    pos_s = jnp.full((nseq,), past_len)
    tabs_q_p = _rope_tables(pos_p, NOPE_D)
    tabs_k_p = _rope_tables(pos_p, 0)
    tabs_q_s = _rope_tables(pos_s, NOPE_D)
    tabs_k_s = _rope_tables(pos_s, 0)
    tri = jnp.tril(jnp.ones((ML_L, ML_L), F32)).astype(BF16)
    scale_row = jnp.full((1, HP), ATTN_SCALE * LOG2E, F32)
    cache_krope_t = cache_krope.transpose(0, 1, 3, 2)

    outs = {k: [] for k in ("ckv_p", "ckv_s", "kr_p", "kr_s", "cv_p", "cv_s",
                            "C_p", "C_s", "n_p", "n_s", "m_p", "m_s")}

    for l in range(depth):
        p = _prep_layer(l, w_in, g_cq, w_uq, g_ckv, w_uk, w_uv, g_q, g_k, b_i, b_f, w_out)
        gmix = g_norm_mix[l][None, :]
        gffn = g_norm_ffn[l][None, :]
        cw = conv_w[l]

        h = inproj(xp, gmix, p["w_in"], tm_p)
        cb, cc, ch, _, _, _, _, _, mo = _split_h(h)
        q = q_path(h, p["g_cq"], p["w_uq"], p["g_q"] * scale_row, tabs_q_p, tm_p, seq // tm_p)
        ckvn, krr = ckv_path(h, p["g_ckv"], tabs_k_p, tm_p, seq // tm_p)
        k, vt = kv_path(ckvn, krr, p["w_k"], p["w_vt"], p["g_k"], batch, seq)
        o_mla = prompt_attention(q.reshape(batch, seq, -1), k.reshape(batch, seq, -1), vt, batch, seq)
        h_ml, c1, n1, m1 = mlstm_prompt(h, p["gate_bias"], tri, batch, seq)

        u = (cc * ch).reshape(batch, seq, CONV_W)
        up = jnp.pad(u, ((0, 0), (CONV_K - 1, 0), (0, 0)))
        conv_out = sum(cw[j] * up[:, j:j + seq] for j in range(CONV_K)).reshape(mp, CONV_W)
        mix = jnp.concatenate([cb * conv_out, o_mla.reshape(mp, MLA_W), _ml_merge(h_ml, mo, g_ml[l])], axis=1)
        xp = matmul_residual(mix.astype(BF16), p["w_out"], xp, tm_p)

        outs["ckv_p"].append(ckvn.reshape(batch, seq, KV_RANK))
        outs["kr_p"].append(krr[:, :ROPE_D].reshape(batch, seq, ROPE_D))
        outs["cv_p"].append(u[:, seq - (CONV_K - 1):])
        outs["C_p"].append(c1)
        outs["n_p"].append(n1.reshape(batch, ML_HEADS, ML_D))
        outs["m_p"].append(m1.reshape(batch, ML_HEADS))

        hs = inproj(xs, gmix, p["w_in"], nseq)
        cb, cc, ch, _, _, mq, mk, mv, mo = _split_h(hs)
        gates = hs[:, IN_MAIN:]
        qs = q_path(hs, p["g_cq"], p["w_uq"], p["g_q"] * p["g_k"] * scale_row, tabs_q_s, nseq, 1)
        ckvn_s, krr_s = ckv_path(hs, p["g_ckv"], tabs_k_s, nseq, 1)
        qs3 = qs.reshape(nseq, MLA_HEADS, HP)
        qlat = bmm(qs3[:, :, :NOPE_D].transpose(1, 0, 2), p["w_uk_h"])
        qpad = ((0, 0), (0, Q_ROWS - MLA_HEADS), (0, 0))
        qlat = jnp.pad(qlat.transpose(1, 0, 2).astype(BF16), qpad)
        qrope = jnp.pad(qs3[:, :, NOPE_D:QK_D], qpad)
        cnew = jnp.pad(ckvn_s[:, None, :], ((0, 0), (0, PAGE_SIZE - 1), (0, 0)))
        rnew_t = jnp.pad(krr_s[:, :ROPE_D, None], ((0, 0), (0, 0), (0, PAGE_SIZE - 1)))
        pc = decode_attention(l, page_table, cache_ckv, cache_krope_t, qlat, qrope, cnew, rnew_t, p["w_ukt_dec"])
        o_s = bmm(pc.transpose(1, 0, 2).astype(BF16), p["w_uv_h"])
        o_s = o_s.transpose(1, 0, 2).reshape(nseq, MLA_W)

        g_all = nseq * ML_HEADS
        li = (gates[:, GATE_I:GATE_I + ML_HEADS] + b_i[l][None, :]).reshape(g_all, 1, 1)
        lfp = (gates[:, GATE_F:GATE_F + ML_HEADS] + b_f[l][None, :]).reshape(g_all, 1, 1)
        h_col, c1s, n1s, m1s = mlstm_step(
            mq.reshape(g_all, 1, ML_D), mk.reshape(g_all, 1, ML_D), mv.reshape(g_all, ML_D, 1), li, lfp,
            state_C[l].reshape(g_all, ML_D, ML_D), state_n[l].reshape(g_all, 1, ML_D),
            state_m[l].reshape(g_all, 1, 1))

        us = cc * ch
        st = state_conv[l]
        conv_s = cw[0] * st[:, 0] + cw[1] * st[:, 1] + cw[2] * us
        mix = jnp.concatenate([cb * conv_s, o_s, _ml_merge(h_col.reshape(nseq, ML_W), mo, g_ml[l])], axis=1)
        xs = matmul_residual(mix.astype(BF16), p["w_out"], xs, nseq)

        outs["ckv_s"].append(ckvn_s[:, None, :])
        outs["kr_s"].append(krr_s[:, None, :ROPE_D])
        outs["cv_s"].append(jnp.stack([st[:, 1], us], axis=1))
        outs["C_s"].append(c1s.reshape(nseq, ML_HEADS, ML_D, ML_D))
        outs["n_s"].append(n1s.reshape(nseq, ML_HEADS, ML_D))
        outs["m_s"].append(m1s.reshape(nseq, ML_HEADS))

        j = l // 2
        if l % 2 == 0:
            w1, w3, w2 = w_ff1[j].astype(BF16), w_ff3[j].astype(BF16), w_ff2[j].astype(BF16)
            xp = ffn_dense(xp, gffn, w1, w3, w2, tm_p, D_FF // 2)
            xs = ffn_dense(xs, gffn, w1, w3, w2, nseq, D_FF // 2)
        else:
            w1, w3, w2 = w_e1[j].astype(BF16), w_e3[j].astype(BF16), w_e2[j].astype(BF16)
            wr = jnp.pad(w_router[j], ((0, 0), (0, LANE - N_EXPERTS)))
            wr_hi = wr.astype(BF16)
            wr_lo = (wr - wr_hi.astype(F32)).astype(BF16)

            xn, logits = router(xp, gffn, wr_hi, wr_lo, tm_p)
            gate, top_i = _route(logits)
            e_flat = top_i.reshape(-1)
            n_assign = mp * TOP_K
            onehot = (e_flat[:, None] == jnp.arange(N_EXPERTS)[None, :]).astype(jnp.int32)
            rank = jnp.take_along_axis(jnp.cumsum(onehot, axis=0) - onehot, e_flat[:, None], axis=1)[:, 0]
            counts = jnp.sum(onehot, axis=0)
            padded = (counts + MOE_TM - 1) // MOE_TM * MOE_TM
            pad_end = jnp.cumsum(padded)
            pad_start = pad_end - padded
            dest = pad_start[e_flat] + rank
            n_blk = (n_assign + N_EXPERTS * (MOE_TM - 1) + MOE_TM - 1) // MOE_TM
            n_slot = n_blk * MOE_TM
            slot_tok = jnp.zeros((n_slot,), jnp.int32).at[dest].set(jnp.arange(n_assign, dtype=jnp.int32) // TOP_K)
            blk_start = jnp.arange(n_blk, dtype=jnp.int32) * MOE_TM
            blk_exp = jnp.minimum(jnp.sum((pad_end[None, :] <= blk_start[:, None]).astype(jnp.int32), axis=1),
                                  N_EXPERTS - 1)
            n_used = (pad_end[-1:] // MOE_TM).astype(jnp.int32)
            yb = moe_blocks(blk_exp, n_used, xn[slot_tok], w1, w3, w2)
            yg = yb[dest].reshape(mp, TOP_K, D_MODEL)
            xp = xp + jnp.sum(yg * gate[:, :, None], axis=1)

            xn_s, logits_s = router(xs, gffn, wr_hi, wr_lo, nseq)
            gate_s, top_s = _route(logits_s)
            dense_gate = jnp.sum(
                (top_s[:, :, None] == jnp.arange(N_EXPERTS)[None, None, :]) * gate_s[:, :, None], axis=1)
            xs = moe_dense(xs, xn_s, dense_gate.T[:, :, None], w1, w3, w2)

    return (xp.reshape(batch, seq, D_MODEL), xs.reshape(nseq, 1, D_MODEL),
            jnp.stack(outs["ckv_p"]), jnp.stack(outs["ckv_s"]),
            jnp.stack(outs["kr_p"]), jnp.stack(outs["kr_s"]),
            jnp.stack(outs["cv_p"]), jnp.stack(outs["cv_s"]),
            jnp.stack(outs["C_p"]), jnp.stack(outs["C_s"]),
            jnp.stack(outs["n_p"]), jnp.stack(outs["n_s"]),
            jnp.stack(outs["m_p"]), jnp.stack(outs["m_s"]))
```

```python
import functools
import math

import jax
import jax.numpy as jnp
import numpy as np
from jax import lax
from jax.experimental import pallas as pl
from jax.experimental.pallas import tpu as pltpu

F32 = jnp.float32
BF16 = jnp.bfloat16

D_MODEL = 1024
PAGE_SIZE = 128
CONV_W = 256
CONV_K = 3
MLA_HEADS = 8
NOPE_D = 64
ROPE_D = 32
HALF_ROPE = ROPE_D // 2
QK_D = NOPE_D + ROPE_D
V_D = 64
MLA_W = MLA_HEADS * V_D
Q_RANK = 384
KV_RANK = 128
ROPE_THETA = 10000.0
ATTN_SCALE = QK_D ** -0.5
ML_HEADS = 4
ML_D = 64
ML_W = ML_HEADS * ML_D
D_FF = 2816
N_EXPERTS = 8
TOP_K = 2
D_FF_EXPERT = 1408
RMS_EPS = 1e-6
LOG2E = math.log2(math.e)

LANE = 128
HP = LANE
IN_MAIN = 2304
IN_PAD = IN_MAIN + LANE
GATE_I = ROPE_D
GATE_F = ROPE_D + ML_HEADS
VMEM_LIMIT = 56 * 1024 * 1024

ATTN_T = 512
ATTN_HG = 2
ATTN_SUM_ROWS = 16
ML_L = 128
DEC_PAGES = 64
DEC_CHUNK = 4
Q_ROWS = 16
MOE_TM = 512


def _cparams(sem):
    return pltpu.CompilerParams(dimension_semantics=sem, vmem_limit_bytes=VMEM_LIMIT)


def _rms(x, g):
    return x * lax.rsqrt(jnp.mean(x * x, axis=-1, keepdims=True) + RMS_EPS) * g


def _dot(a, b):
    return jnp.dot(a, b, preferred_element_type=F32)


def _dot_nt(a, b):
    return lax.dot_general(a, b, (((1,), (1,)), ((), ())), preferred_element_type=F32)


def _split2(x):
    hi = x.astype(BF16)
    lo = (x - hi.astype(F32)).astype(BF16)
    return hi, lo


def _split3(x):
    hi = x.astype(BF16)
    r = x - hi.astype(F32)
    mid = r.astype(BF16)
    lo = (r - mid.astype(F32)).astype(BF16)
    return hi, mid, lo


def _inproj_kernel(x_ref, g_ref, w_ref, o_ref):
    xn = _rms(x_ref[...], g_ref[...]).astype(BF16)
    o_ref[...] = _dot(xn, w_ref[...])


def inproj(x, g, w, tm):
    m = x.shape[0]
    n = w.shape[1]
    return pl.pallas_call(
        _inproj_kernel,
        grid=(m // tm,),
        in_specs=[pl.BlockSpec((tm, D_MODEL), lambda i: (i, 0)),
                  pl.BlockSpec((1, D_MODEL), lambda i: (0, 0)),
                  pl.BlockSpec((D_MODEL, n), lambda i: (0, 0))],
        out_specs=pl.BlockSpec((tm, n), lambda i: (i, 0)),
        out_shape=jax.ShapeDtypeStruct((m, n), F32),
        compiler_params=_cparams(("parallel",)),
        name="inproj",
    )(x, g, w)


def _q_kernel(cq_ref, gcq_ref, wuq_ref, gq_ref, c_ref, s1_ref, s2_ref, o_ref):
    cqn = _rms(cq_ref[...], gcq_ref[...]).astype(BF16)
    q = _dot(cqn, wuq_ref[...])
    c, s1, s2, gq = c_ref[...], s1_ref[...], s2_ref[...], gq_ref[...]
    for h in range(MLA_HEADS):
        qh = q[:, h * HP:(h + 1) * HP]
        qh = qh * c + pltpu.roll(qh, HP - HALF_ROPE, 1) * s1 + pltpu.roll(qh, HALF_ROPE, 1) * s2
        ssq = jnp.sum(qh * qh, axis=-1, keepdims=True)
        qh = qh * lax.rsqrt(ssq * (1.0 / QK_D) + RMS_EPS) * gq
        o_ref[:, h * HP:(h + 1) * HP] = qh.astype(o_ref.dtype)


def q_path(h, gcq, wuq, gq, tabs, tm, n_pos_blk):
    m = h.shape[0]
    cq_blk = (3 * CONV_W) // Q_RANK
    tab_spec = pl.BlockSpec((tm, HP), lambda i: (i % n_pos_blk, 0))
    return pl.pallas_call(
        _q_kernel,
        grid=(m // tm,),
        in_specs=[pl.BlockSpec((tm, Q_RANK), lambda i: (i, cq_blk)),
                  pl.BlockSpec((1, Q_RANK), lambda i: (0, 0)),
                  pl.BlockSpec((Q_RANK, MLA_HEADS * HP), lambda i: (0, 0)),
                  pl.BlockSpec((1, HP), lambda i: (0, 0)),
                  tab_spec, tab_spec, tab_spec],
        out_specs=pl.BlockSpec((tm, MLA_HEADS * HP), lambda i: (i, 0)),
        out_shape=jax.ShapeDtypeStruct((m, MLA_HEADS * HP), BF16),
        compiler_params=_cparams(("parallel",)),
        name="q_path",
    )(h, gcq, wuq, gq, *tabs)


def _ckv_kernel(ckv_ref, kr_ref, g_ref, c_ref, s1_ref, s2_ref, ckvn_ref, krr_ref):
    ckvn_ref[...] = _rms(ckv_ref[...], g_ref[...])
    kr = kr_ref[...]
    krr_ref[...] = (kr * c_ref[...] + pltpu.roll(kr, LANE - HALF_ROPE, 1) * s1_ref[...]
                    + pltpu.roll(kr, HALF_ROPE, 1) * s2_ref[...])


def ckv_path(h, g_ckv, tabs, tm, n_pos_blk):
    m = h.shape[0]
    ckv_blk = (3 * CONV_W + Q_RANK) // KV_RANK
    kr_blk = IN_MAIN // LANE
    tab_spec = pl.BlockSpec((tm, LANE), lambda i: (i % n_pos_blk, 0))
    return pl.pallas_call(
        _ckv_kernel,
        grid=(m // tm,),
        in_specs=[pl.BlockSpec((tm, KV_RANK), lambda i: (i, ckv_blk)),
                  pl.BlockSpec((tm, LANE), lambda i: (i, kr_blk)),
                  pl.BlockSpec((1, KV_RANK), lambda i: (0, 0)),
                  tab_spec, tab_spec, tab_spec],
        out_specs=[pl.BlockSpec((tm, KV_RANK), lambda i: (i, 0)),
                   pl.BlockSpec((tm, LANE), lambda i: (i, 0))],
        out_shape=[jax.ShapeDtypeStruct((m, KV_RANK), F32),
                   jax.ShapeDtypeStruct((m, LANE), F32)],
        compiler_params=_cparams(("parallel",)),
        name="ckv_path",
    )(h, h, g_ckv, *tabs)


def _kv_kernel(ckvn_ref, krr_ref, wk_ref, wvt_ref, gk_ref, k_ref, vt_ref):
    cb = ckvn_ref[...].astype(BF16)
    ck = jnp.concatenate([cb, krr_ref[...].astype(BF16)], axis=1)
    k = _dot(ck, wk_ref[...])
    gk = gk_ref[...]
    for h in range(MLA_HEADS):
        kh = k[:, h * HP:(h + 1) * HP]
        ssq = jnp.sum(kh * kh, axis=-1, keepdims=True)
        k_ref[:, h * HP:(h + 1) * HP] = (kh * lax.rsqrt(ssq * (1.0 / QK_D) + RMS_EPS) * gk).astype(BF16)
    vt_ref[...] = _dot_nt(wvt_ref[...], cb).astype(BF16)


def kv_path(ckvn, krr, wk, wvt, gk, batch, seq):
    tm = ATTN_T
    m = ckvn.shape[0]
    nb = seq // tm
    return pl.pallas_call(
        _kv_kernel,
        grid=(m // tm,),
        in_specs=[pl.BlockSpec((tm, KV_RANK), lambda i: (i, 0)),
                  pl.BlockSpec((tm, LANE), lambda i: (i, 0)),
                  pl.BlockSpec((2 * LANE, MLA_HEADS * HP), lambda i: (0, 0)),
                  pl.BlockSpec((MLA_W, KV_RANK), lambda i: (0, 0)),
                  pl.BlockSpec((1, HP), lambda i: (0, 0))],
        out_specs=[pl.BlockSpec((tm, MLA_HEADS * HP), lambda i: (i, 0)),
                   pl.BlockSpec((None, None, MLA_W, tm), lambda i: (i // nb, i % nb, 0, 0))],
        out_shape=[jax.ShapeDtypeStruct((m, MLA_HEADS * HP), BF16),
                   jax.ShapeDtypeStruct((batch, nb, MLA_W, tm), BF16)],
        compiler_params=_cparams(("parallel",)),
        name="kv_path",
    )(ckvn, krr, wk, wvt, gk)


def _attn_kernel(q_ref, k_ref, vt_ref, o_ref, s0_ref, s1_ref):
    t = ATTN_T
    hg = ATTN_HG
    qi = pl.program_id(2)
    row = lax.broadcasted_iota(jnp.int32, (t, t), 0)
    col = lax.broadcasted_iota(jnp.int32, (t, t), 1)
    causal = row <= col
    ones_rows = jnp.ones((ATTN_SUM_ROWS, t), BF16)

    def qk(j, dst, masked):
        for hh in range(hg):
            k = k_ref[pl.ds(pl.multiple_of(j * t, t), t), hh * HP:(hh + 1) * HP]
            st = _dot_nt(k, q_ref[:, hh * HP:(hh + 1) * HP])
            dst[hh] = jnp.where(causal, st, -jnp.inf) if masked else st

    def process(j, src, carry):
        out = []
        for hh in range(hg):
            m, acc = carry[2 * hh:2 * hh + 2]
            st = src[hh]
            m_new = jnp.maximum(m, jnp.max(st, axis=0, keepdims=True))
            p = jnp.exp2(st - m_new).astype(BF16)
            alpha = jnp.exp2(m - m_new)
            va = jnp.concatenate([vt_ref[j, hh * V_D:(hh + 1) * V_D, :], ones_rows], axis=0)
            out += [m_new, alpha * acc + _dot(va, p)]
        return tuple(out)

    qk(qi, s0_ref, True)
    init = (jnp.full((1, t), -jnp.inf, F32), jnp.zeros((V_D + ATTN_SUM_ROWS, t), F32)) * hg

    def body(tt, carry):
        qk(2 * tt, s1_ref, False)
        carry = process(jnp.where(tt == 0, qi, 2 * tt - 1), s0_ref, carry)
        qk(jnp.minimum(2 * tt + 1, jnp.maximum(qi - 1, 0)), s0_ref, False)
        return process(2 * tt, s1_ref, carry)

    carry = lax.fori_loop(0, (qi + 1) // 2, body, init)
    carry = lax.cond(qi % 2 == 0,
                     lambda c: process(jnp.where(qi == 0, 0, qi - 1), s0_ref, c),
                     lambda c: c, carry)
    o_t = jnp.concatenate([carry[2 * hh + 1][:V_D] / carry[2 * hh + 1][V_D:V_D + 1] for hh in range(hg)], axis=0)
    o_ref[...] = o_t.T


def prompt_attention(q, k, vt, batch, seq):
    t = ATTN_T
    hg = ATTN_HG
    nb = seq // t
    return pl.pallas_call(
        _attn_kernel,
        grid=(batch, MLA_HEADS // hg, nb),
        in_specs=[pl.BlockSpec((None, t, hg * HP), lambda b, p, i: (b, i, p)),
                  pl.BlockSpec((None, seq, hg * HP), lambda b, p, i: (b, 0, p)),
                  pl.BlockSpec((None, nb, hg * V_D, t), lambda b, p, i: (b, 0, p, 0))],
        out_specs=pl.BlockSpec((None, t, hg * V_D), lambda b, p, i: (b, i, p)),
        out_shape=jax.ShapeDtypeStruct((batch, seq, MLA_W), F32),
        scratch_shapes=[pltpu.VMEM((hg, t, t), F32), pltpu.VMEM((hg, t, t), F32)],
        compiler_params=_cparams(("parallel", "parallel", "arbitrary")),
        name="prompt_attention",
    )(q, k, vt)


def _log_sigmoid(x):
    return jnp.minimum(x, 0.0) - jnp.log1p(jnp.exp(-jnp.abs(x)))


def _mlstm_kernel(q_ref, k_ref, v_ref, g_ref, bias_ref, tri_ref,
                  h_ref, c_out, n_out, m_out, c_s, n_s, m_s):
    L = ML_L
    ci = pl.program_id(1)

    @pl.when(ci == 0)
    def _():
        c_s[...] = jnp.zeros_like(c_s)
        n_s[...] = jnp.zeros_like(n_s)
        m_s[...] = jnp.zeros_like(m_s)

    gb = g_ref[...] + bias_ref[...]
    lf = _log_sigmoid(gb)
    hi, mid, lo = _split3(lf)
    tri = tri_ref[...]
    b_all = _dot(tri, hi) + _dot(tri, mid) + _dot(tri, lo)
    b_t = b_all.T
    gb_t = gb.T
    q = q_ref[...]
    k = k_ref[...] * (ML_D ** -0.5)
    v = v_ref[...]
    row = lax.broadcasted_iota(jnp.int32, (L, L), 0)
    col = lax.broadcasted_iota(jnp.int32, (L, L), 1)
    causal = col <= row

    w_end_cols = []
    for h in range(ML_HEADS):
        b_col = b_all[:, GATE_F + h:GATE_F + h + 1]
        li_col = gb[:, GATE_I + h:GATE_I + h + 1]
        b_row = b_t[GATE_F + h:GATE_F + h + 1, :]
        li_row = gb_t[GATE_I + h:GATE_I + h + 1, :]
        m0 = m_s[h]
        c0 = c_s[h]
        n0 = n_s[h]
        qh = q[:, h * ML_D:(h + 1) * ML_D]
        kh = k[:, h * ML_D:(h + 1) * ML_D]
        vh = v[:, h * ML_D:(h + 1) * ML_D]
        qb, kb = qh.astype(BF16), kh.astype(BF16)
        a_col = b_col + m0
        dmat = jnp.where(causal, b_col - b_row + li_row, -jnp.inf)
        m_col = jnp.maximum(a_col, jnp.max(dmat, axis=1, keepdims=True))
        w = jnp.exp(dmat - m_col)
        ws = jnp.exp(a_col - m_col)
        s = _dot_nt(qb, kb) * w
        num = _dot(s.astype(BF16), vh.astype(BF16)) + _dot_nt(qb, c0.astype(BF16)) * ws
        den = jnp.sum(s, axis=1, keepdims=True) + ws * jnp.sum(qh * n0, axis=1, keepdims=True)
        h_ref[:, h * ML_D:(h + 1) * ML_D] = num / jnp.maximum(jnp.abs(den), jnp.exp(-m_col))
        m_end = m_col[L - 1:L, :]
        b_end = b_col[L - 1:L, :]
        w_end = jnp.exp(b_end - b_col + li_col - m_end)
        decay = jnp.exp(b_end + m0 - m_end)
        w_end_cols.append(w_end)
        n_s[h] = decay * n0 + jnp.sum(w_end * kh, axis=0, keepdims=True)
        m_s[h] = m_end
        c_s[h] = decay * c0

    vw = jnp.concatenate([v[:, h * ML_D:(h + 1) * ML_D] * w_end_cols[h] for h in range(ML_HEADS)], axis=1)
    vw_t = vw.T.astype(BF16)
    kb_all = k.astype(BF16)
    for h in range(ML_HEADS):
        c_s[h] = c_s[h] + _dot(vw_t[h * ML_D:(h + 1) * ML_D, :], kb_all[:, h * ML_D:(h + 1) * ML_D])

    @pl.when(ci == pl.num_programs(1) - 1)
    def _():
        c_out[...] = c_s[...]
        n_out[...] = n_s[...]
        m_out[...] = m_s[...]


def mlstm_prompt(h, bias, tri, batch, seq):
    L = ML_L
    nc = seq // L
    qb, kb, vb = (3 * CONV_W + Q_RANK + KV_RANK) // ML_W + np.arange(3)
    gate_blk = IN_MAIN // LANE

    def col(cb):
        return pl.BlockSpec((L, ML_W), lambda b, c: (b * nc + c, cb))

    return pl.pallas_call(
        _mlstm_kernel,
        grid=(batch, nc),
        in_specs=[col(int(qb)), col(int(kb)), col(int(vb)),
                  pl.BlockSpec((L, LANE), lambda b, c: (b * nc + c, gate_blk)),
                  pl.BlockSpec((1, LANE), lambda b, c: (0, 0)),
                  pl.BlockSpec((L, L), lambda b, c: (0, 0))],
        out_specs=[pl.BlockSpec((L, ML_W), lambda b, c: (b * nc + c, 0)),
                   pl.BlockSpec((None, ML_HEADS, ML_D, ML_D), lambda b, c: (b, 0, 0, 0)),
                   pl.BlockSpec((None, ML_HEADS, 1, ML_D), lambda b, c: (b, 0, 0, 0)),
                   pl.BlockSpec((None, ML_HEADS, 1, 1), lambda b, c: (b, 0, 0, 0))],
        out_shape=[jax.ShapeDtypeStruct((batch * seq, ML_W), F32),
                   jax.ShapeDtypeStruct((batch, ML_HEADS, ML_D, ML_D), F32),
                   jax.ShapeDtypeStruct((batch, ML_HEADS, 1, ML_D), F32),
                   jax.ShapeDtypeStruct((batch, ML_HEADS, 1, 1), F32)],
        scratch_shapes=[pltpu.VMEM((ML_HEADS, ML_D, ML_D), F32),
                        pltpu.VMEM((ML_HEADS, 1, ML_D), F32),
                        pltpu.VMEM((ML_HEADS, 1, 1), F32)],
        compiler_params=_cparams(("parallel", "arbitrary")),
        name="mlstm_prompt",
    )(h, h, h, h, bias, tri)


def _mlstm_step_kernel(q_ref, k_ref, vc_ref, li_ref, lfp_ref, c0_ref, n0_ref, m0_ref,
                       h_ref, c1_ref, n1_ref, m1_ref):
    q = q_ref[...]
    k = k_ref[...] * (ML_D ** -0.5)
    vc = vc_ref[...]
    li = li_ref[...]
    lf = _log_sigmoid(lfp_ref[...])
    c0, n0, m0 = c0_ref[...], n0_ref[...], m0_ref[...]
    a = lf + m0
    m = jnp.maximum(a, li)
    w = jnp.exp(li - m)
    ws = jnp.exp(a - m)
    s = jnp.sum(q * k, axis=2, keepdims=True) * w
    cq = jnp.sum(c0 * q, axis=2, keepdims=True)
    num = s * vc + cq * ws
    den = s + ws * jnp.sum(n0 * q, axis=2, keepdims=True)
    h_ref[...] = num / jnp.maximum(jnp.abs(den), jnp.exp(-m))
    c1_ref[...] = ws * c0 + w * (vc * k)
    n1_ref[...] = ws * n0 + w * k
    m1_ref[...] = m


def mlstm_step(q, k, vc, li, lfp, c0, n0, m0):
    g_all = q.shape[0]
    g = 16

    def spec(shape):
        return pl.BlockSpec((g,) + shape, lambda i: (i, 0, 0))

    return pl.pallas_call(
        _mlstm_step_kernel,
        grid=(g_all // g,),
        in_specs=[spec((1, ML_D)), spec((1, ML_D)), spec((ML_D, 1)), spec((1, 1)), spec((1, 1)),
                  spec((ML_D, ML_D)), spec((1, ML_D)), spec((1, 1))],
        out_specs=[spec((ML_D, 1)), spec((ML_D, ML_D)), spec((1, ML_D)), spec((1, 1))],
        out_shape=[jax.ShapeDtypeStruct((g_all, ML_D, 1), F32),
                   jax.ShapeDtypeStruct((g_all, ML_D, ML_D), F32),
                   jax.ShapeDtypeStruct((g_all, 1, ML_D), F32),
                   jax.ShapeDtypeStruct((g_all, 1, 1), F32)],
        compiler_params=_cparams(("parallel",)),
        name="mlstm_step",
    )(q, k, vc, li, lfp, c0, n0, m0)


def _mm_res_kernel(a_ref, b_ref, r_ref, o_ref):
    o_ref[...] = r_ref[...] + _dot(a_ref[...], b_ref[...])


def matmul_residual(a, b, res, tm):
    m, kk = a.shape
    n = b.shape[1]
    return pl.pallas_call(
        _mm_res_kernel,
        grid=(m // tm,),
        in_specs=[pl.BlockSpec((tm, kk), lambda i: (i, 0)),
                  pl.BlockSpec((kk, n), lambda i: (0, 0)),
                  pl.BlockSpec((tm, n), lambda i: (i, 0))],
        out_specs=pl.BlockSpec((tm, n), lambda i: (i, 0)),
        out_shape=jax.ShapeDtypeStruct((m, n), F32),
        compiler_params=_cparams(("parallel",)),
        name="merge_matmul",
    )(a, b, res)


def _bmm_kernel(a_ref, b_ref, o_ref):
    o_ref[...] = _dot(a_ref[...], b_ref[...])


def bmm(a, b):
    nb, m, kk = a.shape
    n = b.shape[2]
    return pl.pallas_call(
        _bmm_kernel,
        grid=(nb,),
        in_specs=[pl.BlockSpec((None, m, kk), lambda i: (i, 0, 0)),
                  pl.BlockSpec((None, kk, n), lambda i: (i, 0, 0))],
        out_specs=pl.BlockSpec((None, m, n), lambda i: (i, 0, 0)),
        out_shape=jax.ShapeDtypeStruct((nb, m, n), F32),
        compiler_params=_cparams(("parallel",)),
        name="bmm",
    )(a, b)


def _decode_kernel(pt_ref, ckv_hbm, kr_hbm, qlat_ref, qrope_ref, cnew_ref, rnew_ref, wukt_ref, o_ref,
                   cbuf, rbuf, sem, m_s, l_s, acc_s, *, layer):
    npg = DEC_PAGES
    nch = npg // DEC_CHUNK
    b, j = pl.program_id(0), pl.program_id(1)
    nj = pl.num_programs(1)
    step = b * nj + j
    last_step = pl.num_programs(0) * nj - 1
    slot = step % 2

    def page_copies(at_step, sl):
        bb, jj = at_step // nj, at_step % nj
        cps = []
        for k in range(npg):
            page = pt_ref[bb, jj * npg + k]
            cps.append(pltpu.make_async_copy(ckv_hbm.at[layer, page],
                                             cbuf.at[sl, pl.ds(k * PAGE_SIZE, PAGE_SIZE)], sem.at[0, sl]))
            cps.append(pltpu.make_async_copy(kr_hbm.at[layer, page], rbuf.at[sl, k], sem.at[1, sl]))
        return cps

    @pl.when(step == 0)
    def _():
        for cp in page_copies(step, slot):
            cp.start()

    for cp in page_copies(step, slot):
        cp.wait()
    prefetch = page_copies(jnp.minimum(step + 1, last_step), 1 - slot)
    per_chunk = len(prefetch) // nch

    @pl.when(j == 0)
    def _():
        m_s[...] = jnp.full_like(m_s, -jnp.inf)
        l_s[...] = jnp.zeros_like(l_s)
        acc_s[...] = jnp.zeros_like(acc_s)

    lhs = jnp.concatenate([wukt_ref[...], qlat_ref[...]], axis=0)
    qrope = qrope_ref[...]
    nk = MLA_HEADS * NOPE_D

    def products(c, rt):
        cb = c.astype(BF16)
        big = _dot_nt(lhs, cb)
        return cb, big, _dot(qrope, rt.astype(BF16)), jnp.sum(rt * rt, axis=0, keepdims=True)

    def weights(prod, valid, m_old):
        cb, big, s_rope, ssq_r = prod
        n = cb.shape[0]
        ssq = jnp.sum(jnp.square(big[:nk].reshape(NOPE_D, MLA_HEADS, n)), axis=0) + ssq_r
        s = (big[nk:nk + MLA_HEADS] + s_rope[:MLA_HEADS]) * lax.rsqrt(ssq * (1.0 / QK_D) + RMS_EPS)
        if valid is not None:
            s = jnp.where(valid, s, -jnp.inf)
        m_new = jnp.maximum(m_old, jnp.max(s, axis=1, keepdims=True))
        return m_new, jnp.exp2(m_old - m_new), jnp.exp2(s - m_new)

    def accumulate(w, cb, l_old, acc):
        _, alpha, p = w
        return alpha * l_old + jnp.sum(p, axis=1, keepdims=True), alpha * acc + _dot(p.astype(BF16), cb)

    rows = DEC_CHUNK * PAGE_SIZE

    def chunk_products(i):
        c = cbuf[slot, pl.ds(i * rows, rows), :]
        rt = jnp.concatenate([rbuf[slot, i * DEC_CHUNK + k] for k in range(DEC_CHUNK)], axis=1)
        return products(c, rt)

    m_run, l_run, acc = m_s[...], l_s[...], acc_s[...]
    prods = [chunk_products(i) for i in range(min(2, nch))]
    for i in range(nch):
        for cp in prefetch[i * per_chunk:(i + 1) * per_chunk]:
            cp.start()
        w = weights(prods[i], None, m_run)
        if i + 2 < nch:
            prods.append(chunk_products(i + 2))
        l_run, acc = accumulate(w, prods[i][0], l_run, acc)
        m_run = w[0]
    m_s[...], l_s[...], acc_s[...] = m_run, l_run, acc

    @pl.when(j == nj - 1)
    def _():
        lane = lax.broadcasted_iota(jnp.int32, (MLA_HEADS, PAGE_SIZE), 1)
        prod = products(cnew_ref[...], rnew_ref[...])
        l_fin, acc_fin = accumulate(weights(prod, lane == 0, m_run), prod[0], l_run, acc)
        o_ref[...] = acc_fin / l_fin

    @pl.when(step == last_step)
    def _():
        for cp in prefetch:
            cp.wait()


def decode_attention(layer, page_table, cache_ckv, cache_krope_t, qlat, qrope, cnew, rnew_t, wukt):
    nseq, n_pages = page_table.shape
    npg = DEC_PAGES
    nj = n_pages // npg

    in_specs = [pl.BlockSpec(memory_space=pl.ANY),
                pl.BlockSpec(memory_space=pl.ANY),
                pl.BlockSpec((None, Q_ROWS, KV_RANK), lambda b, j, pt: (b, 0, 0)),
                pl.BlockSpec((None, Q_ROWS, ROPE_D), lambda b, j, pt: (b, 0, 0)),
                pl.BlockSpec((None, PAGE_SIZE, KV_RANK), lambda b, j, pt: (b, 0, 0)),
                pl.BlockSpec((None, ROPE_D, PAGE_SIZE), lambda b, j, pt: (b, 0, 0)),
                pl.BlockSpec((MLA_HEADS * NOPE_D, KV_RANK), lambda b, j, pt: (0, 0))]
    return pl.pallas_call(
        functools.partial(_decode_kernel, layer=layer),
        grid_spec=pltpu.PrefetchScalarGridSpec(
            num_scalar_prefetch=1,
            grid=(nseq, nj),
            in_specs=in_specs,
            out_specs=pl.BlockSpec((None, MLA_HEADS, KV_RANK), lambda b, j, pt: (b, 0, 0)),
            scratch_shapes=[pltpu.VMEM((2, npg * PAGE_SIZE, KV_RANK), F32),
                            pltpu.VMEM((2, npg, ROPE_D, PAGE_SIZE), F32),
                            pltpu.SemaphoreType.DMA((2, 2)),
                            pltpu.VMEM((MLA_HEADS, 1), F32),
                            pltpu.VMEM((MLA_HEADS, 1), F32),
                            pltpu.VMEM((MLA_HEADS, KV_RANK), F32)]),
        out_shape=jax.ShapeDtypeStruct((nseq, MLA_HEADS, KV_RANK), F32),
        compiler_params=_cparams(("arbitrary", "arbitrary")),
        name="decode_attention",
    )(page_table, cache_ckv, cache_krope_t, qlat, qrope, cnew, rnew_t, wukt)


def _ffn_kernel(x_ref, g_ref, w1_ref, w3_ref, w2_ref, o_ref, xn_ref):
    f = pl.program_id(1)

    @pl.when(f == 0)
    def _():
        x = x_ref[...]
        xn_ref[...] = _rms(x, g_ref[...]).astype(BF16)
        o_ref[...] = x

    xn = xn_ref[...]
    a = _dot(xn, w1_ref[...])
    b = _dot(xn, w3_ref[...])
    act = (a * jax.nn.sigmoid(a) * b).astype(BF16)
    o_ref[...] += _dot(act, w2_ref[...])


def ffn_dense(x, g, w1, w3, w2, tm, tf):
    m = x.shape[0]
    nf = w1.shape[1] // tf
    return pl.pallas_call(
        _ffn_kernel,
        grid=(m // tm, nf),
        in_specs=[pl.BlockSpec((tm, D_MODEL), lambda i, f: (i, 0)),
                  pl.BlockSpec((1, D_MODEL), lambda i, f: (0, 0)),
                  pl.BlockSpec((D_MODEL, tf), lambda i, f: (0, f)),
                  pl.BlockSpec((D_MODEL, tf), lambda i, f: (0, f)),
                  pl.BlockSpec((tf, D_MODEL), lambda i, f: (f, 0))],
        out_specs=pl.BlockSpec((tm, D_MODEL), lambda i, f: (i, 0)),
        out_shape=jax.ShapeDtypeStruct((m, D_MODEL), F32),
        scratch_shapes=[pltpu.VMEM((tm, D_MODEL), BF16)],
        compiler_params=_cparams(("parallel", "arbitrary")),
        name="ffn_dense",
    )(x, g, w1, w3, w2)


def _router_kernel(x_ref, g_ref, wh_ref, wl_ref, info_ref):
    xn = _rms(x_ref[...], g_ref[...])
    hi, lo = _split2(xn)
    lg =_dot(hi, wh_ref[...]) + _dot(hi, wl_ref[...]) + _dot(lo, wh_ref[...])
    lane = lax.broadcasted_iota(jnp.int32, lg.shape, 1).astype(F32)
    lg = jnp.where(lane < N_EXPERTS, lg, -jnp.inf)
    v1 = jnp.max(lg, axis=1, keepdims=True)
    i1 = jnp.min(jnp.where(lg == v1, lane, float(LANE)), axis=1, keepdims=True)
    rest = jnp.where(lane == i1, -jnp.inf, lg)
    v2 = jnp.max(rest, axis=1, keepdims=True)
    i2 = jnp.min(jnp.where(rest == v2, lane, float(LANE)), axis=1, keepdims=True)
    e = jnp.exp(v2 - v1)
    g1 = 1.0 / (1.0 + e)
    info_ref[...] = jnp.where(lane == 0, i1, jnp.where(lane == 1, i2, jnp.where(lane == 2, g1, e * g1)))


def router(x, g, wr_hi, wr_lo, tm):
    m = x.shape[0]
    return pl.pallas_call(
        _router_kernel,
        grid=(m // tm,),
        in_specs=[pl.BlockSpec((tm, D_MODEL), lambda i: (i, 0)),
                  pl.BlockSpec((1, D_MODEL), lambda i: (0, 0)),
                  pl.BlockSpec((D_MODEL, LANE), lambda i: (0, 0)),
                  pl.BlockSpec((D_MODEL, LANE), lambda i: (0, 0))],
        out_specs=pl.BlockSpec((tm, LANE), lambda i: (i, 0)),
        out_shape=jax.ShapeDtypeStruct((m, LANE), F32),
        compiler_params=_cparams(("parallel",)),
        name="router",
    )(x, g, wr_hi, wr_lo)


def _moe_block_kernel(be_ref, nu_ref, x_ref, g_ref, gate_ref, w1_ref, w3_ref, w2_ref, o_ref):
    del be_ref
    i = pl.program_id(0)

    @pl.when(i < nu_ref[0])
    def _():
        x = _rms(x_ref[...], g_ref[...]).astype(BF16)
        a = _dot(x, w1_ref[...])
        b = _dot(x, w3_ref[...])
        act = (a * jax.nn.sigmoid(a) * b).astype(BF16)
        o_ref[...] = _dot(act, w2_ref[...]) * gate_ref[...]

    @pl.when(i >= nu_ref[0])
    def _():
        o_ref[...] = jnp.zeros_like(o_ref)


def moe_blocks(blk_exp, n_used, xg, g, slot_gate, w1, w3, w2):
    n_slot = xg.shape[0]
    tm = MOE_TM
    return pl.pallas_call(
        _moe_block_kernel,
        grid_spec=pltpu.PrefetchScalarGridSpec(
            num_scalar_prefetch=2,
            grid=(n_slot // tm,),
            in_specs=[pl.BlockSpec((tm, D_MODEL), lambda i, be, nu: (i, 0)),
                      pl.BlockSpec((1, D_MODEL), lambda i, be, nu: (0, 0)),
                      pl.BlockSpec((tm, 1), lambda i, be, nu: (i, 0)),
                      pl.BlockSpec((None, D_MODEL, D_FF_EXPERT), lambda i, be, nu: (be[i], 0, 0)),
                      pl.BlockSpec((None, D_MODEL, D_FF_EXPERT), lambda i, be, nu: (be[i], 0, 0)),
                      pl.BlockSpec((None, D_FF_EXPERT, D_MODEL), lambda i, be, nu: (be[i], 0, 0))],
            out_specs=pl.BlockSpec((tm, D_MODEL), lambda i, be, nu: (i, 0))),
        out_shape=jax.ShapeDtypeStruct((n_slot, D_MODEL), F32),
        compiler_params=_cparams(("arbitrary",)),
        name="moe_blocks",
    )(blk_exp, n_used, xg, g, slot_gate, w1, w3, w2)


def _add3_kernel(a_ref, b_ref, c_ref, o_ref):
    o_ref[...] = a_ref[...] + b_ref[...] + c_ref[...]


def add3(a, b, c, tm):
    m, n = a.shape
    spec = pl.BlockSpec((tm, n), lambda i: (i, 0))
    return pl.pallas_call(
        _add3_kernel,
        grid=(m // tm,),
        in_specs=[spec, spec, spec],
        out_specs=spec,
        out_shape=jax.ShapeDtypeStruct((m, n), F32),
        compiler_params=_cparams(("parallel",)),
        name="moe_combine",
    )(a, b, c)


def _moe_dense_kernel(x_ref, g_ref, gate_ref, w1_ref, w3_ref, w2_ref, o_ref):
    e = pl.program_id(0)

    @pl.when(e == 0)
    def _():
        o_ref[...] = x_ref[...]

    xn = _rms(x_ref[...], g_ref[...]).astype(BF16)
    a = _dot(xn, w1_ref[...])
    b = _dot(xn, w3_ref[...])
    act = (a * jax.nn.sigmoid(a) * b).astype(BF16)
    o_ref[...] += _dot(act, w2_ref[...]) * gate_ref[...]


def moe_dense(x, g, gates, w1, w3, w2):
    m = x.shape[0]
    return pl.pallas_call(
        _moe_dense_kernel,
        grid=(N_EXPERTS,),
        in_specs=[pl.BlockSpec((m, D_MODEL), lambda e: (0, 0)),
                  pl.BlockSpec((1, D_MODEL), lambda e: (0, 0)),
                  pl.BlockSpec((None, m, 1), lambda e: (e, 0, 0)),
                  pl.BlockSpec((None, D_MODEL, D_FF_EXPERT), lambda e: (e, 0, 0)),
                  pl.BlockSpec((None, D_MODEL, D_FF_EXPERT), lambda e: (e, 0, 0)),
                  pl.BlockSpec((None, D_FF_EXPERT, D_MODEL), lambda e: (e, 0, 0))],
        out_specs=pl.BlockSpec((m, D_MODEL), lambda e: (0, 0)),
        out_shape=jax.ShapeDtypeStruct((m, D_MODEL), F32),
        compiler_params=_cparams(("arbitrary",)),
        name="moe_dense",
    )(x, g, gates, w1, w3, w2)


def _rope_tables(pos, rope_lane0):
    inv_freq = ROPE_THETA ** (-jnp.arange(HALF_ROPE, dtype=F32) / HALF_ROPE)
    ang = pos.astype(F32)[:, None] * inv_freq[None, :]
    cos, sin = jnp.cos(ang), jnp.sin(ang)
    n = pos.shape[0]
    z = lambda w: jnp.zeros((n, w), F32)
    tail = LANE - rope_lane0 - ROPE_D
    c = jnp.concatenate([jnp.ones((n, rope_lane0), F32), cos, cos, z(tail)], axis=1)
    s1 = jnp.concatenate([z(rope_lane0), -sin, z(HALF_ROPE), z(tail)], axis=1)
    s2 = jnp.concatenate([z(rope_lane0), z(HALF_ROPE), sin, z(tail)], axis=1)
    return c, s1, s2


def _head_gain(g):
    return jnp.concatenate([g[:NOPE_D], g[NOPE_D:], g[NOPE_D:], jnp.zeros((HP - QK_D,), F32)])[None, :]


def _prep_layer(l, w_in, g_cq, w_uq, g_ckv, w_uk, w_uv, g_q, g_k, b_i, b_f, w_out):
    d = {}
    wi = w_in[l]
    kr0 = 3 * CONV_W + Q_RANK + KV_RANK
    d["w_in"] = jnp.concatenate(
        [wi[:, :kr0], wi[:, kr0 + ROPE_D:kr0 + ROPE_D + 4 * ML_W], wi[:, kr0:kr0 + ROPE_D],
         wi[:, kr0 + ROPE_D + 4 * ML_W:], jnp.zeros((D_MODEL, LANE - ROPE_D - 2 * ML_HEADS), F32)],
        axis=1).astype(BF16)
    wq = w_uq[l].reshape(Q_RANK, MLA_HEADS, QK_D)
    d["w_uq"] = jnp.pad(wq, ((0, 0), (0, 0), (0, HP - QK_D))).reshape(Q_RANK, MLA_HEADS * HP).astype(BF16)
    wk = jnp.pad(w_uk[l], ((0, 0), (0, 0), (0, HP - NOPE_D))).reshape(KV_RANK, MLA_HEADS * HP)
    place = jnp.zeros((LANE, MLA_HEADS, HP), F32)
    place = place.at[jnp.arange(ROPE_D)[:, None], jnp.arange(MLA_HEADS)[None, :],
                     NOPE_D + jnp.arange(ROPE_D)[:, None]].set(1.0)
    d["w_k"] = jnp.concatenate([wk, place.reshape(LANE, MLA_HEADS * HP)], axis=0).astype(BF16)
    d["w_vt"] = w_uv[l].reshape(KV_RANK, MLA_W).T.astype(BF16)
    d["w_ukt_dec"] = w_uk[l].transpose(2, 1, 0).reshape(NOPE_D * MLA_HEADS, KV_RANK).astype(BF16)
    d["w_uk_h"] = w_uk[l].transpose(1, 2, 0).astype(BF16)
    d["w_uv_h"] = w_uv[l].transpose(1, 0, 2).astype(BF16)
    d["g_cq"] = g_cq[l][None, :]
    d["g_ckv"] = g_ckv[l][None, :]
    d["g_q"] = _head_gain(g_q[l])
    d["g_k"] = _head_gain(g_k[l])
    bias = jnp.zeros((LANE,), F32).at[GATE_I:GATE_I + ML_HEADS].set(b_i[l])
    d["gate_bias"] = bias.at[GATE_F:GATE_F + ML_HEADS].set(b_f[l])[None, :]
    d["w_out"] = w_out[l].astype(BF16)
    return d


def _split_h(h):
    o = 0
    out = []
    for w in (CONV_W, CONV_W, CONV_W, Q_RANK, KV_RANK, ML_W, ML_W, ML_W, ML_W):
        out.append(h[:, o:o + w])
        o += w
    return out


def _ml_merge(h_ml, mo, g_ml_l):
    hm = h_ml.reshape(-1, ML_HEADS, ML_D)
    ml = hm * lax.rsqrt(jnp.mean(hm * hm, axis=-1, keepdims=True) + RMS_EPS) * g_ml_l[None]
    return ml.reshape(-1, ML_W) * jax.nn.sigmoid(mo)


def _route(info):
    return info[:, TOP_K:2 * TOP_K], info[:, :TOP_K].astype(jnp.int32)


def kernel(x_prompt, x_sample, cache_ckv, cache_krope, page_table, state_conv, state_C, state_n, state_m,
           g_norm_mix, g_norm_ffn, w_in, conv_w, g_cq, w_uq, g_ckv, w_uk, w_uv, g_q, g_k, b_i, b_f, g_ml,
           w_out, w_ff1, w_ff3, w_ff2, w_router, w_e1, w_e3, w_e2):
    batch, seq, _ = x_prompt.shape
    nseq = x_sample.shape[0]
    depth = w_in.shape[0]
    past_len = page_table.shape[1] * PAGE_SIZE
    mp = batch * seq
    tm_p = 512

    xp = x_prompt.reshape(mp, D_MODEL)
    xs = x_sample.reshape(nseq, D_MODEL)

    pos_p = jnp.arange(seq)
    pos_s = jnp.full((nseq,), past_len)
    tabs_q_p = _rope_tables(pos_p, NOPE_D)
    tabs_k_p = _rope_tables(pos_p, 0)
    tabs_q_s = _rope_tables(pos_s, NOPE_D)
    tabs_k_s = _rope_tables(pos_s, 0)
    tri = jnp.tril(jnp.ones((ML_L, ML_L), F32)).astype(BF16)
    scale_row = jnp.full((1, HP), ATTN_SCALE * LOG2E, F32)
    cache_krope_t = cache_krope.transpose(0, 1, 3, 2)

    outs = {k: [] for k in ("ckv_p", "ckv_s", "kr_p", "kr_s", "cv_p", "cv_s",
                            "C_p", "C_s", "n_p", "n_s", "m_p", "m_s")}

    for l in range(depth):
        p = _prep_layer(l, w_in, g_cq, w_uq, g_ckv, w_uk, w_uv, g_q, g_k, b_i, b_f, w_out)
        gmix = g_norm_mix[l][None, :]
        gffn = g_norm_ffn[l][None, :]
        cw = conv_w[l]

        h = inproj(xp, gmix, p["w_in"], tm_p)
        cb, cc, ch, _, _, _, _, _, mo = _split_h(h)
        q = q_path(h, p["g_cq"], p["w_uq"], p["g_q"] * scale_row, tabs_q_p, tm_p, seq // tm_p)
        ckvn, krr = ckv_path(h, p["g_ckv"], tabs_k_p, tm_p, seq // tm_p)
        k, vt = kv_path(ckvn, krr, p["w_k"], p["w_vt"], p["g_k"], batch, seq)
        o_mla = prompt_attention(q.reshape(batch, seq, -1), k.reshape(batch, seq, -1), vt, batch, seq)
        h_ml, c1, n1, m1 = mlstm_prompt(h, p["gate_bias"], tri, batch, seq)

        u = (cc * ch).reshape(batch, seq, CONV_W)
        up = jnp.pad(u, ((0, 0), (CONV_K - 1, 0), (0, 0)))
        conv_out = sum(cw[j] * up[:, j:j + seq] for j in range(CONV_K)).reshape(mp, CONV_W)
        mix = jnp.concatenate([cb * conv_out, o_mla.reshape(mp, MLA_W), _ml_merge(h_ml, mo, g_ml[l])], axis=1)
        xp = matmul_residual(mix.astype(BF16), p["w_out"], xp, tm_p)

        outs["ckv_p"].append(ckvn.reshape(batch, seq, KV_RANK))
        outs["kr_p"].append(krr[:, :ROPE_D].reshape(batch, seq, ROPE_D))
        outs["cv_p"].append(u[:, seq - (CONV_K - 1):])
        outs["C_p"].append(c1)
        outs["n_p"].append(n1.reshape(batch, ML_HEADS, ML_D))
        outs["m_p"].append(m1.reshape(batch, ML_HEADS))

        hs = inproj(xs, gmix, p["w_in"], nseq)
        cb, cc, ch, _, _, mq, mk, mv, mo = _split_h(hs)
        gates = hs[:, IN_MAIN:]
        qs = q_path(hs, p["g_cq"], p["w_uq"], p["g_q"] * p["g_k"] * scale_row, tabs_q_s, nseq, 1)
        ckvn_s, krr_s = ckv_path(hs, p["g_ckv"], tabs_k_s, nseq, 1)
        qs3 = qs.reshape(nseq, MLA_HEADS, HP)
        qlat = bmm(qs3[:, :, :NOPE_D].transpose(1, 0, 2), p["w_uk_h"])
        qpad = ((0, 0), (0, Q_ROWS - MLA_HEADS), (0, 0))
        qlat = jnp.pad(qlat.transpose(1, 0, 2).astype(BF16), qpad)
        qrope = jnp.pad(qs3[:, :, NOPE_D:QK_D], qpad)
        cnew = jnp.pad(ckvn_s[:, None, :], ((0, 0), (0, PAGE_SIZE - 1), (0, 0)))
        rnew_t = jnp.pad(krr_s[:, :ROPE_D, None], ((0, 0), (0, 0), (0, PAGE_SIZE - 1)))
        pc = decode_attention(l, page_table, cache_ckv, cache_krope_t, qlat, qrope, cnew, rnew_t, p["w_ukt_dec"])
        o_s = bmm(pc.transpose(1, 0, 2).astype(BF16), p["w_uv_h"])
        o_s = o_s.transpose(1, 0, 2).reshape(nseq, MLA_W)

        g_all = nseq * ML_HEADS
        li = (gates[:, GATE_I:GATE_I + ML_HEADS] + b_i[l][None, :]).reshape(g_all, 1, 1)
        lfp = (gates[:, GATE_F:GATE_F + ML_HEADS] + b_f[l][None, :]).reshape(g_all, 1, 1)
        h_col, c1s, n1s, m1s = mlstm_step(
            mq.reshape(g_all, 1, ML_D), mk.reshape(g_all, 1, ML_D), mv.reshape(g_all, ML_D, 1), li, lfp,
            state_C[l].reshape(g_all, ML_D, ML_D), state_n[l].reshape(g_all, 1, ML_D),
            state_m[l].reshape(g_all, 1, 1))

        us = cc * ch
        st = state_conv[l]
        conv_s = cw[0] * st[:, 0] + cw[1] * st[:, 1] + cw[2] * us
        mix = jnp.concatenate([cb * conv_s, o_s, _ml_merge(h_col.reshape(nseq, ML_W), mo, g_ml[l])], axis=1)
        xs = matmul_residual(mix.astype(BF16), p["w_out"], xs, nseq)

        outs["ckv_s"].append(ckvn_s[:, None, :])
        outs["kr_s"].append(krr_s[:, None, :ROPE_D])
        outs["cv_s"].append(jnp.stack([st[:, 1], us], axis=1))
        outs["C_s"].append(c1s.reshape(nseq, ML_HEADS, ML_D, ML_D))
        outs["n_s"].append(n1s.reshape(nseq, ML_HEADS, ML_D))
        outs["m_s"].append(m1s.reshape(nseq, ML_HEADS))

        j = l // 2
        if l % 2 == 0:
            w1, w3, w2 = w_ff1[j].astype(BF16), w_ff3[j].astype(BF16), w_ff2[j].astype(BF16)
            xp = ffn_dense(xp, gffn, w1, w3, w2, tm_p, D_FF // 2)
            xs = ffn_dense(xs, gffn, w1, w3, w2, nseq, D_FF // 2)
        else:
            w1, w3, w2 = w_e1[j].astype(BF16), w_e3[j].astype(BF16), w_e2[j].astype(BF16)
            wr = jnp.pad(w_router[j], ((0, 0), (0, LANE - N_EXPERTS)))
            wr_hi = wr.astype(BF16)
            wr_lo = (wr - wr_hi.astype(F32)).astype(BF16)

            gate, top_i = _route(router(xp, gffn, wr_hi, wr_lo, tm_p))
            e_flat = top_i.reshape(-1)
            n_assign = mp * TOP_K
            onehot = (e_flat[:, None] == jnp.arange(N_EXPERTS)[None, :]).astype(jnp.int32)
            rank = jnp.take_along_axis(jnp.cumsum(onehot, axis=0) - onehot, e_flat[:, None], axis=1)[:, 0]
            counts = jnp.sum(onehot, axis=0)
            padded = (counts + MOE_TM - 1) // MOE_TM * MOE_TM
            pad_end = jnp.cumsum(padded)
            pad_start = pad_end - padded
            dest = pad_start[e_flat] + rank
            n_blk = (n_assign + N_EXPERTS * (MOE_TM - 1) + MOE_TM - 1) // MOE_TM
            n_slot = n_blk * MOE_TM
            slot_tok = jnp.zeros((n_slot,), jnp.int32).at[dest].set(jnp.arange(n_assign, dtype=jnp.int32) // TOP_K)
            blk_start = jnp.arange(n_blk, dtype=jnp.int32) * MOE_TM
            blk_exp = jnp.minimum(jnp.sum((pad_end[None, :] <= blk_start[:, None]).astype(jnp.int32), axis=1),
                                  N_EXPERTS - 1)
            n_used = (pad_end[-1:] // MOE_TM).astype(jnp.int32)
            slot_gate = jnp.zeros((n_slot,), F32).at[dest].set(gate.reshape(-1))
            yb = moe_blocks(blk_exp, n_used, xp[slot_tok], gffn, slot_gate[:, None], w1, w3, w2)
            dest2 = dest.reshape(mp, TOP_K)
            xp = add3(xp, yb[dest2[:, 0]], yb[dest2[:, 1]], tm_p)

            gate_s, top_s = _route(router(xs, gffn, wr_hi, wr_lo, nseq))
            dense_gate = jnp.sum(
                (top_s[:, :, None] == jnp.arange(N_EXPERTS)[None, None, :]) * gate_s[:, :, None], axis=1)
            xs = moe_dense(xs, gffn, dense_gate.T[:, :, None], w1, w3, w2)

    return (xp.reshape(batch, seq, D_MODEL), xs.reshape(nseq, 1, D_MODEL),
            jnp.stack(outs["ckv_p"]), jnp.stack(outs["ckv_s"]),
            jnp.stack(outs["kr_p"]), jnp.stack(outs["kr_s"]),
            jnp.stack(outs["cv_p"]), jnp.stack(outs["cv_s"]),
            jnp.stack(outs["C_p"]), jnp.stack(outs["C_s"]),
            jnp.stack(outs["n_p"]), jnp.stack(outs["n_s"]),
            jnp.stack(outs["m_p"]), jnp.stack(outs["m_s"]))
```

```python
import functools
import math

import jax
import jax.numpy as jnp
import numpy as np
from jax import lax
from jax.experimental import pallas as pl
from jax.experimental.pallas import tpu as pltpu

F32 = jnp.float32
BF16 = jnp.bfloat16

D_MODEL = 1024
PAGE_SIZE = 128
CONV_W = 256
CONV_K = 3
MLA_HEADS = 8
NOPE_D = 64
ROPE_D = 32
HALF_ROPE = ROPE_D // 2
QK_D = NOPE_D + ROPE_D
V_D = 64
MLA_W = MLA_HEADS * V_D
Q_RANK = 384
KV_RANK = 128
ROPE_THETA = 10000.0
ATTN_SCALE = QK_D ** -0.5
ML_HEADS = 4
ML_D = 64
ML_W = ML_HEADS * ML_D
D_FF = 2816
N_EXPERTS = 8
TOP_K = 2
D_FF_EXPERT = 1408
RMS_EPS = 1e-6
LOG2E = math.log2(math.e)

LANE = 128
HP = LANE
IN_MAIN = 2304
IN_PAD = IN_MAIN + LANE
GATE_I = ROPE_D
GATE_F = ROPE_D + ML_HEADS
VMEM_LIMIT = 56 * 1024 * 1024

ATTN_T = 512
ATTN_HG = 2
ATTN_SUM_ROWS = 16
ATTN_CK = 256
ML_L = 128
DEC_PAGES = 64
DEC_CHUNK = 4
Q_ROWS = 16
MOE_TM = 512


def _cparams(sem):
    return pltpu.CompilerParams(dimension_semantics=sem, vmem_limit_bytes=VMEM_LIMIT)


def _rms(x, g):
    return x * lax.rsqrt(jnp.mean(x * x, axis=-1, keepdims=True) + RMS_EPS) * g


def _dot(a, b):
    return jnp.dot(a, b, preferred_element_type=F32)


def _dot_nt(a, b):
    return lax.dot_general(a, b, (((1,), (1,)), ((), ())), preferred_element_type=F32)


def _split2(x):
    hi = x.astype(BF16)
    lo = (x - hi.astype(F32)).astype(BF16)
    return hi, lo


def _split3(x):
    hi = x.astype(BF16)
    r = x - hi.astype(F32)
    mid = r.astype(BF16)
    lo = (r - mid.astype(F32)).astype(BF16)
    return hi, mid, lo


def _inproj_kernel(x_ref, g_ref, w_ref, o_ref):
    xn = _rms(x_ref[...], g_ref[...]).astype(BF16)
    o_ref[...] = _dot(xn, w_ref[...])


def inproj(x, g, w, tm):
    m = x.shape[0]
    n = w.shape[1]
    return pl.pallas_call(
        _inproj_kernel,
        grid=(m // tm,),
        in_specs=[pl.BlockSpec((tm, D_MODEL), lambda i: (i, 0)),
                  pl.BlockSpec((1, D_MODEL), lambda i: (0, 0)),
                  pl.BlockSpec((D_MODEL, n), lambda i: (0, 0))],
        out_specs=pl.BlockSpec((tm, n), lambda i: (i, 0)),
        out_shape=jax.ShapeDtypeStruct((m, n), F32),
        compiler_params=_cparams(("parallel",)),
        name="inproj",
    )(x, g, w)


def _q_kernel(cq_ref, gcq_ref, wuq_ref, gq_ref, c_ref, s1_ref, s2_ref, o_ref):
    cqn = _rms(cq_ref[...], gcq_ref[...]).astype(BF16)
    q = _dot(cqn, wuq_ref[...])
    c, s1, s2, gq = c_ref[...], s1_ref[...], s2_ref[...], gq_ref[...]
    for h in range(MLA_HEADS):
        qh = q[:, h * HP:(h + 1) * HP]
        qh = qh * c + pltpu.roll(qh, HP - HALF_ROPE, 1) * s1 + pltpu.roll(qh, HALF_ROPE, 1) * s2
        ssq = jnp.sum(qh * qh, axis=-1, keepdims=True)
        qh = qh * lax.rsqrt(ssq * (1.0 / QK_D) + RMS_EPS) * gq
        o_ref[:, h * HP:(h + 1) * HP] = qh.astype(o_ref.dtype)


def q_path(h, gcq, wuq, gq, tabs, tm, n_pos_blk):
    m = h.shape[0]
    cq_blk = (3 * CONV_W) // Q_RANK
    tab_spec = pl.BlockSpec((tm, HP), lambda i: (i % n_pos_blk, 0))
    return pl.pallas_call(
        _q_kernel,
        grid=(m // tm,),
        in_specs=[pl.BlockSpec((tm, Q_RANK), lambda i: (i, cq_blk)),
                  pl.BlockSpec((1, Q_RANK), lambda i: (0, 0)),
                  pl.BlockSpec((Q_RANK, MLA_HEADS * HP), lambda i: (0, 0)),
                  pl.BlockSpec((1, HP), lambda i: (0, 0)),
                  tab_spec, tab_spec, tab_spec],
        out_specs=pl.BlockSpec((tm, MLA_HEADS * HP), lambda i: (i, 0)),
        out_shape=jax.ShapeDtypeStruct((m, MLA_HEADS * HP), BF16),
        compiler_params=_cparams(("parallel",)),
        name="q_path",
    )(h, gcq, wuq, gq, *tabs)


def _ckv_kernel(ckv_ref, kr_ref, g_ref, c_ref, s1_ref, s2_ref, ckvn_ref, krr_ref):
    ckvn_ref[...] = _rms(ckv_ref[...], g_ref[...])
    kr = kr_ref[...]
    krr_ref[...] = (kr * c_ref[...] + pltpu.roll(kr, LANE - HALF_ROPE, 1) * s1_ref[...]
                    + pltpu.roll(kr, HALF_ROPE, 1) * s2_ref[...])


def ckv_path(h, g_ckv, tabs, tm, n_pos_blk):
    m = h.shape[0]
    ckv_blk = (3 * CONV_W + Q_RANK) // KV_RANK
    kr_blk = IN_MAIN // LANE
    tab_spec = pl.BlockSpec((tm, LANE), lambda i: (i % n_pos_blk, 0))
    return pl.pallas_call(
        _ckv_kernel,
        grid=(m // tm,),
        in_specs=[pl.BlockSpec((tm, KV_RANK), lambda i: (i, ckv_blk)),
                  pl.BlockSpec((tm, LANE), lambda i: (i, kr_blk)),
                  pl.BlockSpec((1, KV_RANK), lambda i: (0, 0)),
                  tab_spec, tab_spec, tab_spec],
        out_specs=[pl.BlockSpec((tm, KV_RANK), lambda i: (i, 0)),
                   pl.BlockSpec((tm, LANE), lambda i: (i, 0))],
        out_shape=[jax.ShapeDtypeStruct((m, KV_RANK), F32),
                   jax.ShapeDtypeStruct((m, LANE), F32)],
        compiler_params=_cparams(("parallel",)),
        name="ckv_path",
    )(h, h, g_ckv, *tabs)


def _kv_kernel(ckvn_ref, krr_ref, wk_ref, wvt_ref, gk_ref, k_ref, vt_ref):
    cb = ckvn_ref[...].astype(BF16)
    ck = jnp.concatenate([cb, krr_ref[...].astype(BF16)], axis=1)
    k = _dot(ck, wk_ref[...])
    gk = gk_ref[...]
    for h in range(MLA_HEADS):
        kh = k[:, h * HP:(h + 1) * HP]
        ssq = jnp.sum(kh * kh, axis=-1, keepdims=True)
        k_ref[:, h * HP:(h + 1) * HP] = (kh * lax.rsqrt(ssq * (1.0 / QK_D) + RMS_EPS) * gk).astype(BF16)
    vt_ref[...] = _dot_nt(wvt_ref[...], cb).astype(BF16)


def kv_path(ckvn, krr, wk, wvt, gk, batch, seq):
    tm = ATTN_T
    m = ckvn.shape[0]
    nb = seq // tm
    return pl.pallas_call(
        _kv_kernel,
        grid=(m // tm,),
        in_specs=[pl.BlockSpec((tm, KV_RANK), lambda i: (i, 0)),
                  pl.BlockSpec((tm, LANE), lambda i: (i, 0)),
                  pl.BlockSpec((2 * LANE, MLA_HEADS * HP), lambda i: (0, 0)),
                  pl.BlockSpec((MLA_W, KV_RANK), lambda i: (0, 0)),
                  pl.BlockSpec((1, HP), lambda i: (0, 0))],
        out_specs=[pl.BlockSpec((tm, MLA_HEADS * HP), lambda i: (i, 0)),
                   pl.BlockSpec((None, None, MLA_W, tm), lambda i: (i // nb, i % nb, 0, 0))],
        out_shape=[jax.ShapeDtypeStruct((m, MLA_HEADS * HP), BF16),
                   jax.ShapeDtypeStruct((batch, nb, MLA_W, tm), BF16)],
        compiler_params=_cparams(("parallel",)),
        name="kv_path",
    )(ckvn, krr, wk, wvt, gk)


def _attn_kernel(q_ref, k_ref, vt_ref, o_ref, s0_ref, s1_ref):
    t = ATTN_T
    hg = ATTN_HG
    qi = pl.program_id(2)
    row = lax.broadcasted_iota(jnp.int32, (t, t), 0)
    col = lax.broadcasted_iota(jnp.int32, (t, t), 1)
    causal = row <= col
    ones_rows = jnp.ones((ATTN_SUM_ROWS, t), BF16)

    def qk(j, dst, masked):
        cms = []
        for hh in range(hg):
            k = k_ref[pl.ds(pl.multiple_of(j * t, t), t), hh * HP:(hh + 1) * HP]
            st = _dot_nt(k, q_ref[:, hh * HP:(hh + 1) * HP])
            st = jnp.where(causal, st, -jnp.inf) if masked else st
            dst[hh] = st
            cms.append(jnp.max(st, axis=0, keepdims=True))
        return tuple(cms)

    def process(j, src, cms, carry):
        out = []
        for hh in range(hg):
            m, acc = carry[2 * hh:2 * hh + 2]
            m_new = jnp.maximum(m, cms[hh])
            p = jnp.exp2(src[hh] - m_new).astype(BF16)
            alpha = jnp.exp2(m - m_new)
            va = jnp.concatenate([vt_ref[j, hh * V_D:(hh + 1) * V_D, :], ones_rows], axis=0)
            out += [m_new, alpha * acc + _dot(va, p)]
        return tuple(out)

    def fused(jq, dst, jp, src, cms, carry):
        ck = ATTN_CK
        out, new_cms = [], []
        for hh in range(hg):
            m, acc = carry[2 * hh:2 * hh + 2]
            m_new = jnp.maximum(m, cms[hh])
            acc = jnp.exp2(m - m_new) * acc
            cm = None
            for c in range(t // ck):
                k = k_ref[pl.ds(pl.multiple_of(jq * t + c * ck, ck), ck), hh * HP:(hh + 1) * HP]
                st = _dot_nt(k, q_ref[:, hh * HP:(hh + 1) * HP])
                dst[hh, c * ck:(c + 1) * ck, :] = st
                cmc = jnp.max(st, axis=0, keepdims=True)
                cm = cmc if cm is None else jnp.maximum(cm, cmc)
                p = jnp.exp2(src[hh, c * ck:(c + 1) * ck, :] - m_new).astype(BF16)
                va = jnp.concatenate([vt_ref[jp, hh * V_D:(hh + 1) * V_D, c * ck:(c + 1) * ck],
                                      ones_rows[:, :ck]], axis=0)
                acc = acc + _dot(va, p)
            new_cms.append(cm)
            out += [m_new, acc]
        return tuple(out), tuple(new_cms)

    ns = 2 * hg
    cm0 = qk(qi, s0_ref, True)
    init = (jnp.full((1, t), -jnp.inf, F32), jnp.zeros((V_D + ATTN_SUM_ROWS, t), F32)) * hg + cm0

    def body(tt, carry):
        state, cm1 = fused(2 * tt, s1_ref, jnp.where(tt == 0, qi, 2 * tt - 1), s0_ref, carry[ns:], carry[:ns])
        state, cm0n = fused(jnp.minimum(2 * tt + 1, jnp.maximum(qi - 1, 0)), s0_ref, 2 * tt, s1_ref, cm1, state)
        return state + cm0n

    carry = lax.fori_loop(0, (qi + 1) // 2, body, init)
    carry = lax.cond(qi % 2 == 0,
                     lambda c: process(jnp.where(qi == 0, 0, qi - 1), s0_ref, c[ns:], c[:ns]),
                     lambda c: c[:ns], carry)
    o_t = jnp.concatenate([carry[2 * hh + 1][:V_D] / carry[2 * hh + 1][V_D:V_D + 1] for hh in range(hg)], axis=0)
    o_ref[...] = o_t.T


def prompt_attention(q, k, vt, batch, seq):
    t = ATTN_T
    hg = ATTN_HG
    nb = seq // t
    return pl.pallas_call(
        _attn_kernel,
        grid=(batch, MLA_HEADS // hg, nb),
        in_specs=[pl.BlockSpec((None, t, hg * HP), lambda b, p, i: (b, i, p)),
                  pl.BlockSpec((None, seq, hg * HP), lambda b, p, i: (b, 0, p)),
                  pl.BlockSpec((None, nb, hg * V_D, t), lambda b, p, i: (b, 0, p, 0))],
        out_specs=pl.BlockSpec((None, t, hg * V_D), lambda b, p, i: (b, i, p)),
        out_shape=jax.ShapeDtypeStruct((batch, seq, MLA_W), F32),
        scratch_shapes=[pltpu.VMEM((hg, t, t), F32), pltpu.VMEM((hg, t, t), F32)],
        compiler_params=_cparams(("parallel", "parallel", "arbitrary")),
        name="prompt_attention",
    )(q, k, vt)


def _log_sigmoid(x):
    return jnp.minimum(x, 0.0) - jnp.log1p(jnp.exp(-jnp.abs(x)))


def _mlstm_kernel(q_ref, k_ref, v_ref, g_ref, bias_ref, tri_ref,
                  h_ref, c_out, n_out, m_out, c_s, n_s, m_s):
    L = ML_L
    ci = pl.program_id(1)

    @pl.when(ci == 0)
    def _():
        c_s[...] = jnp.zeros_like(c_s)
        n_s[...] = jnp.zeros_like(n_s)
        m_s[...] = jnp.zeros_like(m_s)

    gb = g_ref[...] + bias_ref[...]
    lf = _log_sigmoid(gb)
    hi, mid, lo = _split3(lf)
    tri = tri_ref[...]
    b_all = _dot(tri, hi) + _dot(tri, mid) + _dot(tri, lo)
    b_t = b_all.T
    gb_t = gb.T
    q = q_ref[...]
    k = k_ref[...] * (ML_D ** -0.5)
    v = v_ref[...]
    row = lax.broadcasted_iota(jnp.int32, (L, L), 0)
    col = lax.broadcasted_iota(jnp.int32, (L, L), 1)
    causal = col <= row

    w_end_cols = []
    for h in range(ML_HEADS):
        b_col = b_all[:, GATE_F + h:GATE_F + h + 1]
        li_col = gb[:, GATE_I + h:GATE_I + h + 1]
        b_row = b_t[GATE_F + h:GATE_F + h + 1, :]
        li_row = gb_t[GATE_I + h:GATE_I + h + 1, :]
        m0 = m_s[h]
        c0 = c_s[h]
        n0 = n_s[h]
        qh = q[:, h * ML_D:(h + 1) * ML_D]
        kh = k[:, h * ML_D:(h + 1) * ML_D]
        vh = v[:, h * ML_D:(h + 1) * ML_D]
        qb, kb = qh.astype(BF16), kh.astype(BF16)
        a_col = b_col + m0
        dmat = jnp.where(causal, b_col - b_row + li_row, -jnp.inf)
        m_col = jnp.maximum(a_col, jnp.max(dmat, axis=1, keepdims=True))
        w = jnp.exp(dmat - m_col)
        ws = jnp.exp(a_col - m_col)
        s = _dot_nt(qb, kb) * w
        num = _dot(s.astype(BF16), vh.astype(BF16)) + _dot_nt(qb, c0.astype(BF16)) * ws
        den = jnp.sum(s, axis=1, keepdims=True) + ws * jnp.sum(qh * n0, axis=1, keepdims=True)
        h_ref[:, h * ML_D:(h + 1) * ML_D] = num / jnp.maximum(jnp.abs(den), jnp.exp(-m_col))
        m_end = m_col[L - 1:L, :]
        b_end = b_col[L - 1:L, :]
        w_end = jnp.exp(b_end - b_col + li_col - m_end)
        decay = jnp.exp(b_end + m0 - m_end)
        w_end_cols.append(w_end)
        n_s[h] = decay * n0 + jnp.sum(w_end * kh, axis=0, keepdims=True)
        m_s[h] = m_end
        c_s[h] = decay * c0

    vw = jnp.concatenate([v[:, h * ML_D:(h + 1) * ML_D] * w_end_cols[h] for h in range(ML_HEADS)], axis=1)
    vw_t = vw.T.astype(BF16)
    kb_all = k.astype(BF16)
    for h in range(ML_HEADS):
        c_s[h] = c_s[h] + _dot(vw_t[h * ML_D:(h + 1) * ML_D, :], kb_all[:, h * ML_D:(h + 1) * ML_D])

    @pl.when(ci == pl.num_programs(1) - 1)
    def _():
        c_out[...] = c_s[...]
        n_out[...] = n_s[...]
        m_out[...] = m_s[...]


def mlstm_prompt(h, bias, tri, batch, seq):
    L = ML_L
    nc = seq // L
    qb, kb, vb = (3 * CONV_W + Q_RANK + KV_RANK) // ML_W + np.arange(3)
    gate_blk = IN_MAIN // LANE

    def col(cb):
        return pl.BlockSpec((L, ML_W), lambda b, c: (b * nc + c, cb))

    return pl.pallas_call(
        _mlstm_kernel,
        grid=(batch, nc),
        in_specs=[col(int(qb)), col(int(kb)), col(int(vb)),
                  pl.BlockSpec((L, LANE), lambda b, c: (b * nc + c, gate_blk)),
                  pl.BlockSpec((1, LANE), lambda b, c: (0, 0)),
                  pl.BlockSpec((L, L), lambda b, c: (0, 0))],
        out_specs=[pl.BlockSpec((L, ML_W), lambda b, c: (b * nc + c, 0)),
                   pl.BlockSpec((None, ML_HEADS, ML_D, ML_D), lambda b, c: (b, 0, 0, 0)),
                   pl.BlockSpec((None, ML_HEADS, 1, ML_D), lambda b, c: (b, 0, 0, 0)),
                   pl.BlockSpec((None, ML_HEADS, 1, 1), lambda b, c: (b, 0, 0, 0))],
        out_shape=[jax.ShapeDtypeStruct((batch * seq, ML_W), F32),
                   jax.ShapeDtypeStruct((batch, ML_HEADS, ML_D, ML_D), F32),
                   jax.ShapeDtypeStruct((batch, ML_HEADS, 1, ML_D), F32),
                   jax.ShapeDtypeStruct((batch, ML_HEADS, 1, 1), F32)],
        scratch_shapes=[pltpu.VMEM((ML_HEADS, ML_D, ML_D), F32),
                        pltpu.VMEM((ML_HEADS, 1, ML_D), F32),
                        pltpu.VMEM((ML_HEADS, 1, 1), F32)],
        compiler_params=_cparams(("parallel", "arbitrary")),
        name="mlstm_prompt",
    )(h, h, h, h, bias, tri)


def _mlstm_step_kernel(q_ref, k_ref, vc_ref, li_ref, lfp_ref, c0_ref, n0_ref, m0_ref,
                       h_ref, c1_ref, n1_ref, m1_ref):
    q = q_ref[...]
    k = k_ref[...] * (ML_D ** -0.5)
    vc = vc_ref[...]
    li = li_ref[...]
    lf = _log_sigmoid(lfp_ref[...])
    c0, n0, m0 = c0_ref[...], n0_ref[...], m0_ref[...]
    a = lf + m0
    m = jnp.maximum(a, li)
    w = jnp.exp(li - m)
    ws = jnp.exp(a - m)
    s = jnp.sum(q * k, axis=2, keepdims=True) * w
    cq = jnp.sum(c0 * q, axis=2, keepdims=True)
    num = s * vc + cq * ws
    den = s + ws * jnp.sum(n0 * q, axis=2, keepdims=True)
    h_ref[...] = num / jnp.maximum(jnp.abs(den), jnp.exp(-m))
    c1_ref[...] = ws * c0 + w * (vc * k)
    n1_ref[...] = ws * n0 + w * k
    m1_ref[...] = m


def mlstm_step(q, k, vc, li, lfp, c0, n0, m0):
    g_all = q.shape[0]
    g = 16

    def spec(shape):
        return pl.BlockSpec((g,) + shape, lambda i: (i, 0, 0))

    return pl.pallas_call(
        _mlstm_step_kernel,
        grid=(g_all // g,),
        in_specs=[spec((1, ML_D)), spec((1, ML_D)), spec((ML_D, 1)), spec((1, 1)), spec((1, 1)),
                  spec((ML_D, ML_D)), spec((1, ML_D)), spec((1, 1))],
        out_specs=[spec((ML_D, 1)), spec((ML_D, ML_D)), spec((1, ML_D)), spec((1, 1))],
        out_shape=[jax.ShapeDtypeStruct((g_all, ML_D, 1), F32),
                   jax.ShapeDtypeStruct((g_all, ML_D, ML_D), F32),
                   jax.ShapeDtypeStruct((g_all, 1, ML_D), F32),
                   jax.ShapeDtypeStruct((g_all, 1, 1), F32)],
        compiler_params=_cparams(("parallel",)),
        name="mlstm_step",
    )(q, k, vc, li, lfp, c0, n0, m0)


def _mm_res_kernel(a_ref, b_ref, r_ref, o_ref):
    o_ref[...] = r_ref[...] + _dot(a_ref[...], b_ref[...])


def matmul_residual(a, b, res, tm):
    m, kk = a.shape
    n = b.shape[1]
    return pl.pallas_call(
        _mm_res_kernel,
        grid=(m // tm,),
        in_specs=[pl.BlockSpec((tm, kk), lambda i: (i, 0)),
                  pl.BlockSpec((kk, n), lambda i: (0, 0)),
                  pl.BlockSpec((tm, n), lambda i: (i, 0))],
        out_specs=pl.BlockSpec((tm, n), lambda i: (i, 0)),
        out_shape=jax.ShapeDtypeStruct((m, n), F32),
        compiler_params=_cparams(("parallel",)),
        name="merge_matmul",
    )(a, b, res)


def _merge_kernel(cb_ref, cc_ref, ch_ref, mo_ref, hcc_ref, hch_ref, o_ref, hml_ref, cw_ref, gml_ref, w_ref, x_ref,
                  out_ref, *, tiles_per_seq):
    i = pl.program_id(0)
    u = cc_ref[...] * ch_ref[...]
    hu = jnp.where(i % tiles_per_seq == 0, 0.0, hcc_ref[...] * hch_ref[...])
    row = lax.broadcasted_iota(jnp.int32, u.shape, 0)
    prev1 = jnp.where(row == 0, hu[7:8], pltpu.roll(u, 1, 0))
    prev2 = jnp.where(row == 0, hu[6:7], jnp.where(row == 1, hu[7:8], pltpu.roll(u, 2, 0)))
    cw = cw_ref[...]
    conv = cw[0:1] * prev2 + cw[1:2] * prev1 + cw[2:3] * u
    a = (cb_ref[...] * conv).astype(BF16)

    hml = hml_ref[...]
    lane = lax.broadcasted_iota(jnp.int32, hml.shape, 1)
    sq = hml * hml
    scale = jnp.zeros_like(hml)
    for hh in range(ML_HEADS):
        seg = (lane >= hh * ML_D) & (lane < (hh + 1) * ML_D)
        ms = jnp.sum(jnp.where(seg, sq, 0.0), axis=1, keepdims=True) * (1.0 / ML_D)
        scale = jnp.where(seg, lax.rsqrt(ms + RMS_EPS), scale)
    ml = (hml * scale * gml_ref[...] * jax.nn.sigmoid(mo_ref[...])).astype(BF16)

    out_ref[...] = (x_ref[...] + _dot(a, w_ref[0:CONV_W, :])
                    + _dot(o_ref[...].astype(BF16), w_ref[CONV_W:CONV_W + MLA_W, :])
                    + _dot(ml, w_ref[CONV_W + MLA_W:, :]))


def merge_prompt(h, o_mla, h_ml, cw8, g_ml_row, w_out, x, tm, seq):
    m = x.shape[0]
    nb = seq // tm
    mo_blk = (3 * CONV_W + Q_RANK + KV_RANK + 3 * ML_W) // ML_W
    hb = tm // 8

    def col(cb):
        return pl.BlockSpec((tm, CONV_W), lambda i: (i, cb))

    def halo(cb):
        return pl.BlockSpec((8, CONV_W), lambda i: (jnp.maximum(i * hb - 1, 0), cb))

    return pl.pallas_call(
        functools.partial(_merge_kernel, tiles_per_seq=nb),
        grid=(m // tm,),
        in_specs=[col(0), col(1), col(2), col(mo_blk), halo(1), halo(2),
                  pl.BlockSpec((tm, MLA_W), lambda i: (i, 0)),
                  pl.BlockSpec((tm, ML_W), lambda i: (i, 0)),
                  pl.BlockSpec((8, CONV_W), lambda i: (0, 0)),
                  pl.BlockSpec((1, ML_W), lambda i: (0, 0)),
                  pl.BlockSpec((D_MODEL, D_MODEL), lambda i: (0, 0)),
                  pl.BlockSpec((tm, D_MODEL), lambda i: (i, 0))],
        out_specs=pl.BlockSpec((tm, D_MODEL), lambda i: (i, 0)),
        out_shape=jax.ShapeDtypeStruct((m, D_MODEL), F32),
        compiler_params=_cparams(("parallel",)),
        name="merge_prompt",
    )(h, h, h, h, h, h, o_mla, h_ml, cw8, g_ml_row, w_out, x)


def _bmm_kernel(a_ref, b_ref, o_ref):
    o_ref[...] = _dot(a_ref[...], b_ref[...])


def bmm(a, b):
    nb, m, kk = a.shape
    n = b.shape[2]
    return pl.pallas_call(
        _bmm_kernel,
        grid=(nb,),
        in_specs=[pl.BlockSpec((None, m, kk), lambda i: (i, 0, 0)),
                  pl.BlockSpec((None, kk, n), lambda i: (i, 0, 0))],
        out_specs=pl.BlockSpec((None, m, n), lambda i: (i, 0, 0)),
        out_shape=jax.ShapeDtypeStruct((nb, m, n), F32),
        compiler_params=_cparams(("parallel",)),
        name="bmm",
    )(a, b)


def _decode_kernel(pt_ref, ckv_hbm, kr_hbm, qlat_ref, qrope_ref, cnew_ref, rnew_ref, wukt_ref, o_ref,
                   cbuf, rbuf, sem, m_s, l_s, acc_s, *, layer):
    npg = DEC_PAGES
    nch = npg // DEC_CHUNK
    b, j = pl.program_id(0), pl.program_id(1)
    nj = pl.num_programs(1)
    step = b * nj + j
    last_step = pl.num_programs(0) * nj - 1
    slot = step % 2

    def page_copies(at_step, sl):
        bb, jj = at_step // nj, at_step % nj
        cps = []
        for k in range(npg):
            page = pt_ref[bb, jj * npg + k]
            cps.append(pltpu.make_async_copy(ckv_hbm.at[layer, page],
                                             cbuf.at[sl, pl.ds(k * PAGE_SIZE, PAGE_SIZE)], sem.at[0, sl]))
            cps.append(pltpu.make_async_copy(kr_hbm.at[layer, page], rbuf.at[sl, k], sem.at[1, sl]))
        return cps

    @pl.when(step == 0)
    def _():
        for cp in page_copies(step, slot):
            cp.start()

    for cp in page_copies(step, slot):
        cp.wait()
    prefetch = page_copies(jnp.minimum(step + 1, last_step), 1 - slot)
    per_chunk = 2 * len(prefetch) // nch

    @pl.when(j == 0)
    def _():
        m_s[...] = jnp.full_like(m_s, -jnp.inf)
        l_s[...] = jnp.zeros_like(l_s)
        acc_s[...] = jnp.zeros_like(acc_s)

    lhs = jnp.concatenate([wukt_ref[...], qlat_ref[...]], axis=0)
    qrope = qrope_ref[...]
    nk = MLA_HEADS * NOPE_D

    def products(c, rt):
        cb = c.astype(BF16)
        big = _dot_nt(lhs, cb)
        return cb, big, _dot(qrope, rt.astype(BF16)), jnp.sum(rt * rt, axis=0, keepdims=True)

    def weights(prod, valid, m_old):
        cb, big, s_rope, ssq_r = prod
        n = cb.shape[0]
        ssq = jnp.sum(jnp.square(big[:nk].reshape(NOPE_D, MLA_HEADS, n)), axis=0) + ssq_r
        s = (big[nk:nk + MLA_HEADS] + s_rope[:MLA_HEADS]) * lax.rsqrt(ssq * (1.0 / QK_D) + RMS_EPS)
        if valid is not None:
            s = jnp.where(valid, s, -jnp.inf)
        m_new = jnp.maximum(m_old, jnp.max(s, axis=1, keepdims=True))
        return m_new, jnp.exp2(m_old - m_new), jnp.exp2(s - m_new)

    def accumulate(w, cb, l_old, acc):
        _, alpha, p = w
        return alpha * l_old + jnp.sum(p, axis=1, keepdims=True), alpha * acc + _dot(p.astype(BF16), cb)

    rows = DEC_CHUNK * PAGE_SIZE

    def chunk_products(i):
        c = cbuf[slot, pl.ds(i * rows, rows), :]
        rt = jnp.concatenate([rbuf[slot, i * DEC_CHUNK + k] for k in range(DEC_CHUNK)], axis=1)
        return products(c, rt)

    m_run, l_run, acc = m_s[...], l_s[...], acc_s[...]
    prods = [chunk_products(i) for i in range(min(2, nch))]
    for i in range(nch):
        for cp in prefetch[i * per_chunk:(i + 1) * per_chunk]:
            cp.start()
        w = weights(prods[i], None, m_run)
        if i + 2 < nch:
            prods.append(chunk_products(i + 2))
        l_run, acc = accumulate(w, prods[i][0], l_run, acc)
        m_run = w[0]
    m_s[...], l_s[...], acc_s[...] = m_run, l_run, acc

    @pl.when(j == nj - 1)
    def _():
        lane = lax.broadcasted_iota(jnp.int32, (MLA_HEADS, PAGE_SIZE), 1)
        prod = products(cnew_ref[...], rnew_ref[...])
        l_fin, acc_fin = accumulate(weights(prod, lane == 0, m_run), prod[0], l_run, acc)
        o_ref[...] = acc_fin / l_fin

    @pl.when(step == last_step)
    def _():
        for cp in prefetch:
            cp.wait()


def decode_attention(layer, page_table, cache_ckv, cache_krope_t, qlat, qrope, cnew, rnew_t, wukt):
    nseq, n_pages = page_table.shape
    npg = DEC_PAGES
    nj = n_pages // npg

    in_specs = [pl.BlockSpec(memory_space=pl.ANY),
                pl.BlockSpec(memory_space=pl.ANY),
                pl.BlockSpec((None, Q_ROWS, KV_RANK), lambda b, j, pt: (b, 0, 0)),
                pl.BlockSpec((None, Q_ROWS, ROPE_D), lambda b, j, pt: (b, 0, 0)),
                pl.BlockSpec((None, PAGE_SIZE, KV_RANK), lambda b, j, pt: (b, 0, 0)),
                pl.BlockSpec((None, ROPE_D, PAGE_SIZE), lambda b, j, pt: (b, 0, 0)),
                pl.BlockSpec((MLA_HEADS * NOPE_D, KV_RANK), lambda b, j, pt: (0, 0))]
    return pl.pallas_call(
        functools.partial(_decode_kernel, layer=layer),
        grid_spec=pltpu.PrefetchScalarGridSpec(
            num_scalar_prefetch=1,
            grid=(nseq, nj),
            in_specs=in_specs,
            out_specs=pl.BlockSpec((None, MLA_HEADS, KV_RANK), lambda b, j, pt: (b, 0, 0)),
            scratch_shapes=[pltpu.VMEM((2, npg * PAGE_SIZE, KV_RANK), F32),
                            pltpu.VMEM((2, npg, ROPE_D, PAGE_SIZE), F32),
                            pltpu.SemaphoreType.DMA((2, 2)),
                            pltpu.VMEM((MLA_HEADS, 1), F32),
                            pltpu.VMEM((MLA_HEADS, 1), F32),
                            pltpu.VMEM((MLA_HEADS, KV_RANK), F32)]),
        out_shape=jax.ShapeDtypeStruct((nseq, MLA_HEADS, KV_RANK), F32),
        compiler_params=_cparams(("arbitrary", "arbitrary")),
        name="decode_attention",
    )(page_table, cache_ckv, cache_krope_t, qlat, qrope, cnew, rnew_t, wukt)


def _ffn_kernel(x_ref, g_ref, w1_ref, w3_ref, w2_ref, o_ref, xn_ref):
    f = pl.program_id(1)

    @pl.when(f == 0)
    def _():
        x = x_ref[...]
        xn_ref[...] = _rms(x, g_ref[...]).astype(BF16)
        o_ref[...] = x

    xn = xn_ref[...]
    a = _dot(xn, w1_ref[...])
    b = _dot(xn, w3_ref[...])
    act = (a * jax.nn.sigmoid(a) * b).astype(BF16)
    o_ref[...] += _dot(act, w2_ref[...])


def ffn_dense(x, g, w1, w3, w2, tm, tf):
    m = x.shape[0]
    nf = w1.shape[1] // tf
    return pl.pallas_call(
        _ffn_kernel,
        grid=(m // tm, nf),
        in_specs=[pl.BlockSpec((tm, D_MODEL), lambda i, f: (i, 0)),
                  pl.BlockSpec((1, D_MODEL), lambda i, f: (0, 0)),
                  pl.BlockSpec((D_MODEL, tf), lambda i, f: (0, f)),
                  pl.BlockSpec((D_MODEL, tf), lambda i, f: (0, f)),
                  pl.BlockSpec((tf, D_MODEL), lambda i, f: (f, 0))],
        out_specs=pl.BlockSpec((tm, D_MODEL), lambda i, f: (i, 0)),
        out_shape=jax.ShapeDtypeStruct((m, D_MODEL), F32),
        scratch_shapes=[pltpu.VMEM((tm, D_MODEL), BF16)],
        compiler_params=_cparams(("parallel", "arbitrary")),
        name="ffn_dense",
    )(x, g, w1, w3, w2)


def _router_kernel(x_ref, g_ref, wh_ref, wl_ref, info_ref):
    xn = _rms(x_ref[...], g_ref[...])
    hi, lo = _split2(xn)
    lg =_dot(hi, wh_ref[...]) + _dot(hi, wl_ref[...]) + _dot(lo, wh_ref[...])
    lane = lax.broadcasted_iota(jnp.int32, lg.shape, 1).astype(F32)
    lg = jnp.where(lane < N_EXPERTS, lg, -jnp.inf)
    v1 = jnp.max(lg, axis=1, keepdims=True)
    i1 = jnp.min(jnp.where(lg == v1, lane, float(LANE)), axis=1, keepdims=True)
    rest = jnp.where(lane == i1, -jnp.inf, lg)
    v2 = jnp.max(rest, axis=1, keepdims=True)
    i2 = jnp.min(jnp.where(rest == v2, lane, float(LANE)), axis=1, keepdims=True)
    e = jnp.exp(v2 - v1)
    g1 = 1.0 / (1.0 + e)
    info_ref[...] = jnp.where(lane == 0, i1, jnp.where(lane == 1, i2, jnp.where(lane == 2, g1, e * g1)))


def router(x, g, wr_hi, wr_lo, tm):
    m = x.shape[0]
    return pl.pallas_call(
        _router_kernel,
        grid=(m // tm,),
        in_specs=[pl.BlockSpec((tm, D_MODEL), lambda i: (i, 0)),
                  pl.BlockSpec((1, D_MODEL), lambda i: (0, 0)),
                  pl.BlockSpec((D_MODEL, LANE), lambda i: (0, 0)),
                  pl.BlockSpec((D_MODEL, LANE), lambda i: (0, 0))],
        out_specs=pl.BlockSpec((tm, LANE), lambda i: (i, 0)),
        out_shape=jax.ShapeDtypeStruct((m, LANE), F32),
        compiler_params=_cparams(("parallel",)),
        name="router",
    )(x, g, wr_hi, wr_lo)


def _moe_block_kernel(be_ref, nu_ref, x_ref, g_ref, gate_ref, w1_ref, w3_ref, w2_ref, o_ref):
    del be_ref
    i = pl.program_id(0)

    @pl.when(i < nu_ref[0])
    def _():
        x = _rms(x_ref[...], g_ref[...]).astype(BF16)
        a = _dot(x, w1_ref[...])
        b = _dot(x, w3_ref[...])
        act = (a * jax.nn.sigmoid(a) * b).astype(BF16)
        o_ref[...] = _dot(act, w2_ref[...]) * gate_ref[...]

    @pl.when(i >= nu_ref[0])
    def _():
        o_ref[...] = jnp.zeros_like(o_ref)


def moe_blocks(blk_exp, n_used, xg, g, slot_gate, w1, w3, w2):
    n_slot = xg.shape[0]
    tm = MOE_TM
    return pl.pallas_call(
        _moe_block_kernel,
        grid_spec=pltpu.PrefetchScalarGridSpec(
            num_scalar_prefetch=2,
            grid=(n_slot // tm,),
            in_specs=[pl.BlockSpec((tm, D_MODEL), lambda i, be, nu: (i, 0)),
                      pl.BlockSpec((1, D_MODEL), lambda i, be, nu: (0, 0)),
                      pl.BlockSpec((tm, 1), lambda i, be, nu: (i, 0)),
                      pl.BlockSpec((None, D_MODEL, D_FF_EXPERT), lambda i, be, nu: (be[i], 0, 0)),
                      pl.BlockSpec((None, D_MODEL, D_FF_EXPERT), lambda i, be, nu: (be[i], 0, 0)),
                      pl.BlockSpec((None, D_FF_EXPERT, D_MODEL), lambda i, be, nu: (be[i], 0, 0))],
            out_specs=pl.BlockSpec((tm, D_MODEL), lambda i, be, nu: (i, 0))),
        out_shape=jax.ShapeDtypeStruct((n_slot, D_MODEL), F32),
        compiler_params=_cparams(("arbitrary",)),
        name="moe_blocks",
    )(blk_exp, n_used, xg, g, slot_gate, w1, w3, w2)


def _add3_kernel(a_ref, b_ref, c_ref, o_ref):
    o_ref[...] = a_ref[...] + b_ref[...] + c_ref[...]


def add3(a, b, c, tm):
    m, n = a.shape
    spec = pl.BlockSpec((tm, n), lambda i: (i, 0))
    return pl.pallas_call(
        _add3_kernel,
        grid=(m // tm,),
        in_specs=[spec, spec, spec],
        out_specs=spec,
        out_shape=jax.ShapeDtypeStruct((m, n), F32),
        compiler_params=_cparams(("parallel",)),
        name="moe_combine",
    )(a, b, c)


def _moe_dense_kernel(x_ref, g_ref, gate_ref, w1_ref, w3_ref, w2_ref, o_ref):
    e = pl.program_id(0)

    @pl.when(e == 0)
    def _():
        o_ref[...] = x_ref[...]

    xn = _rms(x_ref[...], g_ref[...]).astype(BF16)
    a = _dot(xn, w1_ref[...])
    b = _dot(xn, w3_ref[...])
    act = (a * jax.nn.sigmoid(a) * b).astype(BF16)
    o_ref[...] += _dot(act, w2_ref[...]) * gate_ref[...]


def moe_dense(x, g, gates, w1, w3, w2):
    m = x.shape[0]
    return pl.pallas_call(
        _moe_dense_kernel,
        grid=(N_EXPERTS,),
        in_specs=[pl.BlockSpec((m, D_MODEL), lambda e: (0, 0)),
                  pl.BlockSpec((1, D_MODEL), lambda e: (0, 0)),
                  pl.BlockSpec((None, m, 1), lambda e: (e, 0, 0)),
                  pl.BlockSpec((None, D_MODEL, D_FF_EXPERT), lambda e: (e, 0, 0)),
                  pl.BlockSpec((None, D_MODEL, D_FF_EXPERT), lambda e: (e, 0, 0)),
                  pl.BlockSpec((None, D_FF_EXPERT, D_MODEL), lambda e: (e, 0, 0))],
        out_specs=pl.BlockSpec((m, D_MODEL), lambda e: (0, 0)),
        out_shape=jax.ShapeDtypeStruct((m, D_MODEL), F32),
        compiler_params=_cparams(("arbitrary",)),
        name="moe_dense",
    )(x, g, gates, w1, w3, w2)


def _rope_tables(pos, rope_lane0):
    inv_freq = ROPE_THETA ** (-jnp.arange(HALF_ROPE, dtype=F32) / HALF_ROPE)
    ang = pos.astype(F32)[:, None] * inv_freq[None, :]
    cos, sin = jnp.cos(ang), jnp.sin(ang)
    n = pos.shape[0]
    z = lambda w: jnp.zeros((n, w), F32)
    tail = LANE - rope_lane0 - ROPE_D
    c = jnp.concatenate([jnp.ones((n, rope_lane0), F32), cos, cos, z(tail)], axis=1)
    s1 = jnp.concatenate([z(rope_lane0), -sin, z(HALF_ROPE), z(tail)], axis=1)
    s2 = jnp.concatenate([z(rope_lane0), z(HALF_ROPE), sin, z(tail)], axis=1)
    return c, s1, s2


def _head_gain(g):
    return jnp.concatenate([g[:NOPE_D], g[NOPE_D:], g[NOPE_D:], jnp.zeros((HP - QK_D,), F32)])[None, :]


def _prep_layer(l, w_in, g_cq, w_uq, g_ckv, w_uk, w_uv, g_q, g_k, b_i, b_f, w_out):
    d = {}
    wi = w_in[l]
    kr0 = 3 * CONV_W + Q_RANK + KV_RANK
    d["w_in"] = jnp.concatenate(
        [wi[:, :kr0], wi[:, kr0 + ROPE_D:kr0 + ROPE_D + 4 * ML_W], wi[:, kr0:kr0 + ROPE_D],
         wi[:, kr0 + ROPE_D + 4 * ML_W:], jnp.zeros((D_MODEL, LANE - ROPE_D - 2 * ML_HEADS), F32)],
        axis=1).astype(BF16)
    wq = w_uq[l].reshape(Q_RANK, MLA_HEADS, QK_D)
    d["w_uq"] = jnp.pad(wq, ((0, 0), (0, 0), (0, HP - QK_D))).reshape(Q_RANK, MLA_HEADS * HP).astype(BF16)
    wk = jnp.pad(w_uk[l], ((0, 0), (0, 0), (0, HP - NOPE_D))).reshape(KV_RANK, MLA_HEADS * HP)
    place = jnp.zeros((LANE, MLA_HEADS, HP), F32)
    place = place.at[jnp.arange(ROPE_D)[:, None], jnp.arange(MLA_HEADS)[None, :],
                     NOPE_D + jnp.arange(ROPE_D)[:, None]].set(1.0)
    d["w_k"] = jnp.concatenate([wk, place.reshape(LANE, MLA_HEADS * HP)], axis=0).astype(BF16)
    d["w_vt"] = w_uv[l].reshape(KV_RANK, MLA_W).T.astype(BF16)
    d["w_ukt_dec"] = w_uk[l].transpose(2, 1, 0).reshape(NOPE_D * MLA_HEADS, KV_RANK).astype(BF16)
    d["w_uk_h"] = w_uk[l].transpose(1, 2, 0).astype(BF16)
    d["w_uv_h"] = w_uv[l].transpose(1, 0, 2).astype(BF16)
    d["g_cq"] = g_cq[l][None, :]
    d["g_ckv"] = g_ckv[l][None, :]
    d["g_q"] = _head_gain(g_q[l])
    d["g_k"] = _head_gain(g_k[l])
    bias = jnp.zeros((LANE,), F32).at[GATE_I:GATE_I + ML_HEADS].set(b_i[l])
    d["gate_bias"] = bias.at[GATE_F:GATE_F + ML_HEADS].set(b_f[l])[None, :]
    d["w_out"] = w_out[l].astype(BF16)
    return d


def _split_h(h):
    o = 0
    out = []
    for w in (CONV_W, CONV_W, CONV_W, Q_RANK, KV_RANK, ML_W, ML_W, ML_W, ML_W):
        out.append(h[:, o:o + w])
        o += w
    return out


def _ml_merge(h_ml, mo, g_ml_l):
    hm = h_ml.reshape(-1, ML_HEADS, ML_D)
    ml = hm * lax.rsqrt(jnp.mean(hm * hm, axis=-1, keepdims=True) + RMS_EPS) * g_ml_l[None]
    return ml.reshape(-1, ML_W) * jax.nn.sigmoid(mo)


def _route(info):
    return info[:, TOP_K:2 * TOP_K], info[:, :TOP_K].astype(jnp.int32)


def kernel(x_prompt, x_sample, cache_ckv, cache_krope, page_table, state_conv, state_C, state_n, state_m,
           g_norm_mix, g_norm_ffn, w_in, conv_w, g_cq, w_uq, g_ckv, w_uk, w_uv, g_q, g_k, b_i, b_f, g_ml,
           w_out, w_ff1, w_ff3, w_ff2, w_router, w_e1, w_e3, w_e2):
    batch, seq, _ = x_prompt.shape
    nseq = x_sample.shape[0]
    depth = w_in.shape[0]
    past_len = page_table.shape[1] * PAGE_SIZE
    mp = batch * seq
    tm_p = 512

    xp = x_prompt.reshape(mp, D_MODEL)
    xs = x_sample.reshape(nseq, D_MODEL)

    pos_p = jnp.arange(seq)
    pos_s = jnp.full((nseq,), past_len)
    tabs_q_p = _rope_tables(pos_p, NOPE_D)
    tabs_k_p = _rope_tables(pos_p, 0)
    tabs_q_s = _rope_tables(pos_s, NOPE_D)
    tabs_k_s = _rope_tables(pos_s, 0)
    tri = jnp.tril(jnp.ones((ML_L, ML_L), F32)).astype(BF16)
    scale_row = jnp.full((1, HP), ATTN_SCALE * LOG2E, F32)
    cache_krope_t = cache_krope.transpose(0, 1, 3, 2)

    outs = {k: [] for k in ("ckv_p", "ckv_s", "kr_p", "kr_s", "cv_p", "cv_s",
                            "C_p", "C_s", "n_p", "n_s", "m_p", "m_s")}

    for l in range(depth):
        p = _prep_layer(l, w_in, g_cq, w_uq, g_ckv, w_uk, w_uv, g_q, g_k, b_i, b_f, w_out)
        gmix = g_norm_mix[l][None, :]
        gffn = g_norm_ffn[l][None, :]
        cw = conv_w[l]

        h = inproj(xp, gmix, p["w_in"], tm_p)
        q = q_path(h, p["g_cq"], p["w_uq"], p["g_q"] * scale_row, tabs_q_p, tm_p, seq // tm_p)
        ckvn, krr = ckv_path(h, p["g_ckv"], tabs_k_p, tm_p, seq // tm_p)
        k, vt = kv_path(ckvn, krr, p["w_k"], p["w_vt"], p["g_k"], batch, seq)
        o_mla = prompt_attention(q.reshape(batch, seq, -1), k.reshape(batch, seq, -1), vt, batch, seq)
        h_ml, c1, n1, m1 = mlstm_prompt(h, p["gate_bias"], tri, batch, seq)

        cw8 = jnp.pad(cw, ((0, 8 - CONV_K), (0, 0)))
        xp = merge_prompt(h, o_mla.reshape(mp, MLA_W), h_ml, cw8, g_ml[l].reshape(1, ML_W), p["w_out"], xp,
                          tm_p, seq)
        tail = h.reshape(batch, seq, IN_PAD)[:, seq - (CONV_K - 1):, CONV_W:3 * CONV_W]

        outs["ckv_p"].append(ckvn.reshape(batch, seq, KV_RANK))
        outs["kr_p"].append(krr[:, :ROPE_D].reshape(batch, seq, ROPE_D))
        outs["cv_p"].append(tail[..., :CONV_W] * tail[..., CONV_W:])
        outs["C_p"].append(c1)
        outs["n_p"].append(n1.reshape(batch, ML_HEADS, ML_D))
        outs["m_p"].append(m1.reshape(batch, ML_HEADS))

        hs = inproj(xs, gmix, p["w_in"], nseq)
        cb, cc, ch, _, _, mq, mk, mv, mo = _split_h(hs)
        gates = hs[:, IN_MAIN:]
        qs = q_path(hs, p["g_cq"], p["w_uq"], p["g_q"] * p["g_k"] * scale_row, tabs_q_s, nseq, 1)
        ckvn_s, krr_s = ckv_path(hs, p["g_ckv"], tabs_k_s, nseq, 1)
        qs3 = qs.reshape(nseq, MLA_HEADS, HP)
        qlat = bmm(qs3[:, :, :NOPE_D].transpose(1, 0, 2), p["w_uk_h"])
        qpad = ((0, 0), (0, Q_ROWS - MLA_HEADS), (0, 0))
        qlat = jnp.pad(qlat.transpose(1, 0, 2).astype(BF16), qpad)
        qrope = jnp.pad(qs3[:, :, NOPE_D:QK_D], qpad)
        cnew = jnp.pad(ckvn_s[:, None, :], ((0, 0), (0, PAGE_SIZE - 1), (0, 0)))
        rnew_t = jnp.pad(krr_s[:, :ROPE_D, None], ((0, 0), (0, 0), (0, PAGE_SIZE - 1)))
        pc = decode_attention(l, page_table, cache_ckv, cache_krope_t, qlat, qrope, cnew, rnew_t, p["w_ukt_dec"])
        o_s = bmm(pc.transpose(1, 0, 2).astype(BF16), p["w_uv_h"])
        o_s = o_s.transpose(1, 0, 2).reshape(nseq, MLA_W)

        g_all = nseq * ML_HEADS
        li = (gates[:, GATE_I:GATE_I + ML_HEADS] + b_i[l][None, :]).reshape(g_all, 1, 1)
        lfp = (gates[:, GATE_F:GATE_F + ML_HEADS] + b_f[l][None, :]).reshape(g_all, 1, 1)
        h_col, c1s, n1s, m1s = mlstm_step(
            mq.reshape(g_all, 1, ML_D), mk.reshape(g_all, 1, ML_D), mv.reshape(g_all, ML_D, 1), li, lfp,
            state_C[l].reshape(g_all, ML_D, ML_D), state_n[l].reshape(g_all, 1, ML_D),
            state_m[l].reshape(g_all, 1, 1))

        us = cc * ch
        st = state_conv[l]
        conv_s = cw[0] * st[:, 0] + cw[1] * st[:, 1] + cw[2] * us
        mix = jnp.concatenate([cb * conv_s, o_s, _ml_merge(h_col.reshape(nseq, ML_W), mo, g_ml[l])], axis=1)
        xs = matmul_residual(mix.astype(BF16), p["w_out"], xs, nseq)

        outs["ckv_s"].append(ckvn_s[:, None, :])
        outs["kr_s"].append(krr_s[:, None, :ROPE_D])
        outs["cv_s"].append(jnp.stack([st[:, 1], us], axis=1))
        outs["C_s"].append(c1s.reshape(nseq, ML_HEADS, ML_D, ML_D))
        outs["n_s"].append(n1s.reshape(nseq, ML_HEADS, ML_D))
        outs["m_s"].append(m1s.reshape(nseq, ML_HEADS))

        j = l // 2
        if l % 2 == 0:
            w1, w3, w2 = w_ff1[j].astype(BF16), w_ff3[j].astype(BF16), w_ff2[j].astype(BF16)
            xp = ffn_dense(xp, gffn, w1, w3, w2, tm_p, D_FF // 2)
            xs = ffn_dense(xs, gffn, w1, w3, w2, nseq, D_FF // 2)
        else:
            w1, w3, w2 = w_e1[j].astype(BF16), w_e3[j].astype(BF16), w_e2[j].astype(BF16)
            wr = jnp.pad(w_router[j], ((0, 0), (0, LANE - N_EXPERTS)))
            wr_hi = wr.astype(BF16)
            wr_lo = (wr - wr_hi.astype(F32)).astype(BF16)

            gate, top_i = _route(router(xp, gffn, wr_hi, wr_lo, tm_p))
            e_flat = top_i.reshape(-1)
            n_assign = mp * TOP_K
            onehot = (e_flat[:, None] == jnp.arange(N_EXPERTS)[None, :]).astype(jnp.int32)
            rank = jnp.take_along_axis(jnp.cumsum(onehot, axis=0) - onehot, e_flat[:, None], axis=1)[:, 0]
            counts = jnp.sum(onehot, axis=0)
            padded = (counts + MOE_TM - 1) // MOE_TM * MOE_TM
            pad_end = jnp.cumsum(padded)
            pad_start = pad_end - padded
            dest = pad_start[e_flat] + rank
            n_blk = (n_assign + N_EXPERTS * (MOE_TM - 1) + MOE_TM - 1) // MOE_TM
            n_slot = n_blk * MOE_TM
            slot_asg = jnp.full((n_slot,), -1, jnp.int32).at[dest].set(jnp.arange(n_assign, dtype=jnp.int32))
            slot_tok = jnp.maximum(slot_asg, 0) // TOP_K
            slot_gate = jnp.where(slot_asg >= 0, gate.reshape(-1)[jnp.maximum(slot_asg, 0)], 0.0)
            blk_start = jnp.arange(n_blk, dtype=jnp.int32) * MOE_TM
            blk_exp = jnp.minimum(jnp.sum((pad_end[None, :] <= blk_start[:, None]).astype(jnp.int32), axis=1),
                                  N_EXPERTS - 1)
            n_used = (pad_end[-1:] // MOE_TM).astype(jnp.int32)
            yb = moe_blocks(blk_exp, n_used, xp[slot_tok], gffn, slot_gate[:, None], w1, w3, w2)
            dest2 = dest.reshape(mp, TOP_K)
            xp = add3(xp, yb[dest2[:, 0]], yb[dest2[:, 1]], tm_p)

            gate_s, top_s = _route(router(xs, gffn, wr_hi, wr_lo, nseq))
            dense_gate = jnp.sum(
                (top_s[:, :, None] == jnp.arange(N_EXPERTS)[None, None, :]) * gate_s[:, :, None], axis=1)
            xs = moe_dense(xs, gffn, dense_gate.T[:, :, None], w1, w3, w2)

    return (xp.reshape(batch, seq, D_MODEL), xs.reshape(nseq, 1, D_MODEL),
            jnp.stack(outs["ckv_p"]), jnp.stack(outs["ckv_s"]),
            jnp.stack(outs["kr_p"]), jnp.stack(outs["kr_s"]),
            jnp.stack(outs["cv_p"]), jnp.stack(outs["cv_s"]),
            jnp.stack(outs["C_p"]), jnp.stack(outs["C_s"]),
            jnp.stack(outs["n_p"]), jnp.stack(outs["n_s"]),
            jnp.stack(outs["m_p"]), jnp.stack(outs["m_s"]))
```

```python
import functools
import math

import jax
import jax.numpy as jnp
import numpy as np
from jax import lax
from jax.experimental import pallas as pl
from jax.experimental.pallas import tpu as pltpu

F32 = jnp.float32
BF16 = jnp.bfloat16

D_MODEL = 1024
PAGE_SIZE = 128
CONV_W = 256
CONV_K = 3
MLA_HEADS = 8
NOPE_D = 64
ROPE_D = 32
HALF_ROPE = ROPE_D // 2
QK_D = NOPE_D + ROPE_D
V_D = 64
MLA_W = MLA_HEADS * V_D
Q_RANK = 384
KV_RANK = 128
ROPE_THETA = 10000.0
ATTN_SCALE = QK_D ** -0.5
ML_HEADS = 4
ML_D = 64
ML_W = ML_HEADS * ML_D
D_FF = 2816
N_EXPERTS = 8
TOP_K = 2
D_FF_EXPERT = 1408
RMS_EPS = 1e-6
LOG2E = math.log2(math.e)

LANE = 128
HP = LANE
IN_MAIN = 2304
IN_PAD = IN_MAIN + LANE
GATE_I = ROPE_D
GATE_F = ROPE_D + ML_HEADS
VMEM_LIMIT = 56 * 1024 * 1024

ATTN_T = 512
ATTN_HG = 2
ATTN_SUM_ROWS = 16
ATTN_CK = 256
ML_L = 128
DEC_PAGES = 128
DEC_CHUNK = 4
Q_ROWS = 16
MOE_TM = 512


def _cparams(sem):
    return pltpu.CompilerParams(dimension_semantics=sem, vmem_limit_bytes=VMEM_LIMIT)


def _rms(x, g):
    return x * lax.rsqrt(jnp.mean(x * x, axis=-1, keepdims=True) + RMS_EPS) * g


def _dot(a, b):
    return jnp.dot(a, b, preferred_element_type=F32)


def _dot_nt(a, b):
    return lax.dot_general(a, b, (((1,), (1,)), ((), ())), preferred_element_type=F32)


def _split2(x):
    hi = x.astype(BF16)
    lo = (x - hi.astype(F32)).astype(BF16)
    return hi, lo


def _split3(x):
    hi = x.astype(BF16)
    r = x - hi.astype(F32)
    mid = r.astype(BF16)
    lo = (r - mid.astype(F32)).astype(BF16)
    return hi, mid, lo


def _inproj_kernel(x_ref, g_ref, w_ref, o_ref):
    xn = _rms(x_ref[...], g_ref[...]).astype(BF16)
    o_ref[...] = _dot(xn, w_ref[...])


def inproj(x, g, w, tm):
    m = x.shape[0]
    n = w.shape[1]
    return pl.pallas_call(
        _inproj_kernel,
        grid=(m // tm,),
        in_specs=[pl.BlockSpec((tm, D_MODEL), lambda i: (i, 0)),
                  pl.BlockSpec((1, D_MODEL), lambda i: (0, 0)),
                  pl.BlockSpec((D_MODEL, n), lambda i: (0, 0))],
        out_specs=pl.BlockSpec((tm, n), lambda i: (i, 0)),
        out_shape=jax.ShapeDtypeStruct((m, n), F32),
        compiler_params=_cparams(("parallel",)),
        name="inproj",
    )(x, g, w)


def _q_kernel(cq_ref, gcq_ref, wuq_ref, gq_ref, c_ref, s1_ref, s2_ref, o_ref):
    cqn = _rms(cq_ref[...], gcq_ref[...]).astype(BF16)
    q = _dot(cqn, wuq_ref[...])
    c, s1, s2, gq = c_ref[...], s1_ref[...], s2_ref[...], gq_ref[...]
    for h in range(MLA_HEADS):
        qh = q[:, h * HP:(h + 1) * HP]
        qh = qh * c + pltpu.roll(qh, HP - HALF_ROPE, 1) * s1 + pltpu.roll(qh, HALF_ROPE, 1) * s2
        ssq = jnp.sum(qh * qh, axis=-1, keepdims=True)
        qh = qh * lax.rsqrt(ssq * (1.0 / QK_D) + RMS_EPS) * gq
        o_ref[:, h * HP:(h + 1) * HP] = qh.astype(o_ref.dtype)


def q_path(h, gcq, wuq, gq, tabs, tm, n_pos_blk):
    m = h.shape[0]
    cq_blk = (3 * CONV_W) // Q_RANK
    tab_spec = pl.BlockSpec((tm, HP), lambda i: (i % n_pos_blk, 0))
    return pl.pallas_call(
        _q_kernel,
        grid=(m // tm,),
        in_specs=[pl.BlockSpec((tm, Q_RANK), lambda i: (i, cq_blk)),
                  pl.BlockSpec((1, Q_RANK), lambda i: (0, 0)),
                  pl.BlockSpec((Q_RANK, MLA_HEADS * HP), lambda i: (0, 0)),
                  pl.BlockSpec((1, HP), lambda i: (0, 0)),
                  tab_spec, tab_spec, tab_spec],
        out_specs=pl.BlockSpec((tm, MLA_HEADS * HP), lambda i: (i, 0)),
        out_shape=jax.ShapeDtypeStruct((m, MLA_HEADS * HP), BF16),
        compiler_params=_cparams(("parallel",)),
        name="q_path",
    )(h, gcq, wuq, gq, *tabs)


def _ckv_kernel(ckv_ref, kr_ref, g_ref, c_ref, s1_ref, s2_ref, ckvn_ref, krr_ref):
    ckvn_ref[...] = _rms(ckv_ref[...], g_ref[...])
    kr = kr_ref[...]
    krr_ref[...] = (kr * c_ref[...] + pltpu.roll(kr, LANE - HALF_ROPE, 1) * s1_ref[...]
                    + pltpu.roll(kr, HALF_ROPE, 1) * s2_ref[...])


def ckv_path(h, g_ckv, tabs, tm, n_pos_blk):
    m = h.shape[0]
    ckv_blk = (3 * CONV_W + Q_RANK) // KV_RANK
    kr_blk = IN_MAIN // LANE
    tab_spec = pl.BlockSpec((tm, LANE), lambda i: (i % n_pos_blk, 0))
    return pl.pallas_call(
        _ckv_kernel,
        grid=(m // tm,),
        in_specs=[pl.BlockSpec((tm, KV_RANK), lambda i: (i, ckv_blk)),
                  pl.BlockSpec((tm, LANE), lambda i: (i, kr_blk)),
                  pl.BlockSpec((1, KV_RANK), lambda i: (0, 0)),
                  tab_spec, tab_spec, tab_spec],
        out_specs=[pl.BlockSpec((tm, KV_RANK), lambda i: (i, 0)),
                   pl.BlockSpec((tm, LANE), lambda i: (i, 0))],
        out_shape=[jax.ShapeDtypeStruct((m, KV_RANK), F32),
                   jax.ShapeDtypeStruct((m, LANE), F32)],
        compiler_params=_cparams(("parallel",)),
        name="ckv_path",
    )(h, h, g_ckv, *tabs)


def _kv_kernel(ckvn_ref, krr_ref, wk_ref, wvt_ref, gk_ref, k_ref, vt_ref):
    cb = ckvn_ref[...].astype(BF16)
    ck = jnp.concatenate([cb, krr_ref[...].astype(BF16)], axis=1)
    k = _dot(ck, wk_ref[...])
    gk = gk_ref[...]
    for h in range(MLA_HEADS):
        kh = k[:, h * HP:(h + 1) * HP]
        ssq = jnp.sum(kh * kh, axis=-1, keepdims=True)
        k_ref[:, h * HP:(h + 1) * HP] = (kh * lax.rsqrt(ssq * (1.0 / QK_D) + RMS_EPS) * gk).astype(BF16)
    vt_ref[...] = _dot_nt(wvt_ref[...], cb).astype(BF16)


def kv_path(ckvn, krr, wk, wvt, gk, batch, seq):
    tm = ATTN_T
    m = ckvn.shape[0]
    nb = seq // tm
    return pl.pallas_call(
        _kv_kernel,
        grid=(m // tm,),
        in_specs=[pl.BlockSpec((tm, KV_RANK), lambda i: (i, 0)),
                  pl.BlockSpec((tm, LANE), lambda i: (i, 0)),
                  pl.BlockSpec((2 * LANE, MLA_HEADS * HP), lambda i: (0, 0)),
                  pl.BlockSpec((MLA_W, KV_RANK), lambda i: (0, 0)),
                  pl.BlockSpec((1, HP), lambda i: (0, 0))],
        out_specs=[pl.BlockSpec((tm, MLA_HEADS * HP), lambda i: (i, 0)),
                   pl.BlockSpec((None, None, MLA_W, tm), lambda i: (i // nb, i % nb, 0, 0))],
        out_shape=[jax.ShapeDtypeStruct((m, MLA_HEADS * HP), BF16),
                   jax.ShapeDtypeStruct((batch, nb, MLA_W, tm), BF16)],
        compiler_params=_cparams(("parallel",)),
        name="kv_path",
    )(ckvn, krr, wk, wvt, gk)


def _attn_kernel(q_ref, k_ref, vt_ref, o_ref, s0_ref, s1_ref):
    t = ATTN_T
    hg = ATTN_HG
    qi = pl.program_id(2)
    row = lax.broadcasted_iota(jnp.int32, (t, t), 0)
    col = lax.broadcasted_iota(jnp.int32, (t, t), 1)
    causal = row <= col
    ones_rows = jnp.ones((ATTN_SUM_ROWS, t), BF16)

    def qk(j, dst, masked):
        cms = []
        for hh in range(hg):
            k = k_ref[pl.ds(pl.multiple_of(j * t, t), t), hh * HP:(hh + 1) * HP]
            st = _dot_nt(k, q_ref[:, hh * HP:(hh + 1) * HP])
            st = jnp.where(causal, st, -jnp.inf) if masked else st
            dst[hh] = st
            cms.append(jnp.max(st, axis=0, keepdims=True))
        return tuple(cms)

    def process(j, src, cms, carry):
        out = []
        for hh in range(hg):
            m, acc = carry[2 * hh:2 * hh + 2]
            m_new = jnp.maximum(m, cms[hh])
            p = jnp.exp2(src[hh] - m_new).astype(BF16)
            alpha = jnp.exp2(m - m_new)
            va = jnp.concatenate([vt_ref[j, hh * V_D:(hh + 1) * V_D, :], ones_rows], axis=0)
            out += [m_new, alpha * acc + _dot(va, p)]
        return tuple(out)

    def fused(jq, dst, jp, src, cms, carry):
        ck = ATTN_CK
        out, new_cms = [], []
        for hh in range(hg):
            m, acc = carry[2 * hh:2 * hh + 2]
            m_new = jnp.maximum(m, cms[hh])
            acc = jnp.exp2(m - m_new) * acc
            k = k_ref[pl.ds(pl.multiple_of(jq * t, t), t), hh * HP:(hh + 1) * HP]
            st = _dot_nt(k, q_ref[:, hh * HP:(hh + 1) * HP])
            dst[hh] = st
            cm = jnp.max(st, axis=0, keepdims=True)
            for c in range(t // ck):
                p = jnp.exp2(src[hh, c * ck:(c + 1) * ck, :] - m_new).astype(BF16)
                va = jnp.concatenate([vt_ref[jp, hh * V_D:(hh + 1) * V_D, c * ck:(c + 1) * ck],
                                      ones_rows[:, :ck]], axis=0)
                acc = acc + _dot(va, p)
            new_cms.append(cm)
            out += [m_new, acc]
        return tuple(out), tuple(new_cms)

    ns = 2 * hg
    cm0 = qk(qi, s0_ref, True)
    init = (jnp.full((1, t), -jnp.inf, F32), jnp.zeros((V_D + ATTN_SUM_ROWS, t), F32)) * hg + cm0

    def body(tt, carry):
        state, cm1 = fused(2 * tt, s1_ref, jnp.where(tt == 0, qi, 2 * tt - 1), s0_ref, carry[ns:], carry[:ns])
        state, cm0n = fused(jnp.minimum(2 * tt + 1, jnp.maximum(qi - 1, 0)), s0_ref, 2 * tt, s1_ref, cm1, state)
        return state + cm0n

    carry = lax.fori_loop(0, (qi + 1) // 2, body, init)
    carry = lax.cond(qi % 2 == 0,
                     lambda c: process(jnp.where(qi == 0, 0, qi - 1), s0_ref, c[ns:], c[:ns]),
                     lambda c: c[:ns], carry)
    o_t = jnp.concatenate([carry[2 * hh + 1][:V_D] / carry[2 * hh + 1][V_D:V_D + 1] for hh in range(hg)], axis=0)
    o_ref[...] = o_t.T


def prompt_attention(q, k, vt, batch, seq):
    t = ATTN_T
    hg = ATTN_HG
    nb = seq // t
    return pl.pallas_call(
        _attn_kernel,
        grid=(batch, MLA_HEADS // hg, nb),
        in_specs=[pl.BlockSpec((None, t, hg * HP), lambda b, p, i: (b, i, p)),
                  pl.BlockSpec((None, seq, hg * HP), lambda b, p, i: (b, 0, p)),
                  pl.BlockSpec((None, nb, hg * V_D, t), lambda b, p, i: (b, 0, p, 0))],
        out_specs=pl.BlockSpec((None, t, hg * V_D), lambda b, p, i: (b, i, p)),
        out_shape=jax.ShapeDtypeStruct((batch, seq, MLA_W), F32),
        scratch_shapes=[pltpu.VMEM((hg, t, t), F32), pltpu.VMEM((hg, t, t), F32)],
        compiler_params=_cparams(("parallel", "parallel", "arbitrary")),
        name="prompt_attention",
    )(q, k, vt)


def _log_sigmoid(x):
    return jnp.minimum(x, 0.0) - jnp.log1p(jnp.exp(-jnp.abs(x)))


def _mlstm_kernel(q_ref, k_ref, v_ref, g_ref, bias_ref, tri_ref,
                  h_ref, c_out, n_out, m_out, c_s, n_s, m_s):
    L = ML_L
    ci = pl.program_id(1)

    @pl.when(ci == 0)
    def _():
        c_s[...] = jnp.zeros_like(c_s)
        n_s[...] = jnp.zeros_like(n_s)
        m_s[...] = jnp.zeros_like(m_s)

    gb = g_ref[...] + bias_ref[...]
    lf = _log_sigmoid(gb)
    hi, mid, lo = _split3(lf)
    tri = tri_ref[...]
    b_all = _dot(tri, hi) + _dot(tri, mid) + _dot(tri, lo)
    b_t = b_all.T
    gb_t = gb.T
    q = q_ref[...]
    k = k_ref[...] * (ML_D ** -0.5)
    v = v_ref[...]
    row = lax.broadcasted_iota(jnp.int32, (L, L), 0)
    col = lax.broadcasted_iota(jnp.int32, (L, L), 1)
    causal = col <= row

    heads = range(ML_HEADS)
    sl = [slice(h * ML_D, (h + 1) * ML_D) for h in heads]
    qb_all, kb_all, vb_all = q.astype(BF16), k.astype(BF16), v.astype(BF16)
    m0 = [m_s[h] for h in heads]
    c0 = [c_s[h] for h in heads]
    n0 = [n_s[h] for h in heads]
    qk = [_dot_nt(qb_all[:, sl[h]], kb_all[:, sl[h]]) for h in heads]
    qc = [_dot_nt(qb_all[:, sl[h]], c0[h].astype(BF16)) for h in heads]
    b_col = [b_all[:, GATE_F + h:GATE_F + h + 1] for h in heads]
    li_col = [gb[:, GATE_I + h:GATE_I + h + 1] for h in heads]
    a_col = [b_col[h] + m0[h] for h in heads]
    dmat = [jnp.where(causal, b_col[h] - b_t[GATE_F + h:GATE_F + h + 1, :] + gb_t[GATE_I + h:GATE_I + h + 1, :],
                      -jnp.inf) for h in heads]
    m_col = [jnp.maximum(a_col[h], jnp.max(dmat[h], axis=1, keepdims=True)) for h in heads]
    ws = [jnp.exp(a_col[h] - m_col[h]) for h in heads]
    s = [qk[h] * jnp.exp(dmat[h] - m_col[h]) for h in heads]
    qn = [jnp.sum(q[:, sl[h]] * n0[h], axis=1, keepdims=True) for h in heads]
    num = [_dot(s[h].astype(BF16), vb_all[:, sl[h]]) + qc[h] * ws[h] for h in heads]
    den = [jnp.sum(s[h], axis=1, keepdims=True) + ws[h] * qn[h] for h in heads]
    for h in heads:
        h_ref[:, sl[h]] = num[h] / jnp.maximum(jnp.abs(den[h]), jnp.exp(-m_col[h]))
    m_end = [m_col[h][L - 1:L, :] for h in heads]
    b_end = [b_col[h][L - 1:L, :] for h in heads]
    w_end = [jnp.exp(b_end[h] - b_col[h] + li_col[h] - m_end[h]) for h in heads]
    decay = [jnp.exp(b_end[h] + m0[h] - m_end[h]) for h in heads]
    vw_t = jnp.concatenate([v[:, sl[h]] * w_end[h] for h in heads], axis=1).T.astype(BF16)
    for h in heads:
        n_s[h] = decay[h] * n0[h] + jnp.sum(w_end[h] * k[:, sl[h]], axis=0, keepdims=True)
        m_s[h] = m_end[h]
        c_s[h] = decay[h] * c0[h] + _dot(vw_t[sl[h], :], kb_all[:, sl[h]])

    @pl.when(ci == pl.num_programs(1) - 1)
    def _():
        c_out[...] = c_s[...]
        n_out[...] = n_s[...]
        m_out[...] = m_s[...]


def mlstm_prompt(h, bias, tri, batch, seq):
    L = ML_L
    nc = seq // L
    qb, kb, vb = (3 * CONV_W + Q_RANK + KV_RANK) // ML_W + np.arange(3)
    gate_blk = IN_MAIN // LANE

    def col(cb):
        return pl.BlockSpec((L, ML_W), lambda b, c: (b * nc + c, cb))

    return pl.pallas_call(
        _mlstm_kernel,
        grid=(batch, nc),
        in_specs=[col(int(qb)), col(int(kb)), col(int(vb)),
                  pl.BlockSpec((L, LANE), lambda b, c: (b * nc + c, gate_blk)),
                  pl.BlockSpec((1, LANE), lambda b, c: (0, 0)),
                  pl.BlockSpec((L, L), lambda b, c: (0, 0))],
        out_specs=[pl.BlockSpec((L, ML_W), lambda b, c: (b * nc + c, 0)),
                   pl.BlockSpec((None, ML_HEADS, ML_D, ML_D), lambda b, c: (b, 0, 0, 0)),
                   pl.BlockSpec((None, ML_HEADS, 1, ML_D), lambda b, c: (b, 0, 0, 0)),
                   pl.BlockSpec((None, ML_HEADS, 1, 1), lambda b, c: (b, 0, 0, 0))],
        out_shape=[jax.ShapeDtypeStruct((batch * seq, ML_W), F32),
                   jax.ShapeDtypeStruct((batch, ML_HEADS, ML_D, ML_D), F32),
                   jax.ShapeDtypeStruct((batch, ML_HEADS, 1, ML_D), F32),
                   jax.ShapeDtypeStruct((batch, ML_HEADS, 1, 1), F32)],
        scratch_shapes=[pltpu.VMEM((ML_HEADS, ML_D, ML_D), F32),
                        pltpu.VMEM((ML_HEADS, 1, ML_D), F32),
                        pltpu.VMEM((ML_HEADS, 1, 1), F32)],
        compiler_params=_cparams(("parallel", "arbitrary")),
        name="mlstm_prompt",
    )(h, h, h, h, bias, tri)


def _mlstm_step_kernel(q_ref, k_ref, vc_ref, li_ref, lfp_ref, c0_ref, n0_ref, m0_ref,
                       h_ref, c1_ref, n1_ref, m1_ref):
    q = q_ref[...]
    k = k_ref[...] * (ML_D ** -0.5)
    vc = vc_ref[...]
    li = li_ref[...]
    lf = _log_sigmoid(lfp_ref[...])
    c0, n0, m0 = c0_ref[...], n0_ref[...], m0_ref[...]
    a = lf + m0
    m = jnp.maximum(a, li)
    w = jnp.exp(li - m)
    ws = jnp.exp(a - m)
    s = jnp.sum(q * k, axis=2, keepdims=True) * w
    cq = jnp.sum(c0 * q, axis=2, keepdims=True)
    num = s * vc + cq * ws
    den = s + ws * jnp.sum(n0 * q, axis=2, keepdims=True)
    h_ref[...] = num / jnp.maximum(jnp.abs(den), jnp.exp(-m))
    c1_ref[...] = ws * c0 + w * (vc * k)
    n1_ref[...] = ws * n0 + w * k
    m1_ref[...] = m


def mlstm_step(q, k, vc, li, lfp, c0, n0, m0):
    g_all = q.shape[0]
    g = 16

    def spec(shape):
        return pl.BlockSpec((g,) + shape, lambda i: (i, 0, 0))

    return pl.pallas_call(
        _mlstm_step_kernel,
        grid=(g_all // g,),
        in_specs=[spec((1, ML_D)), spec((1, ML_D)), spec((ML_D, 1)), spec((1, 1)), spec((1, 1)),
                  spec((ML_D, ML_D)), spec((1, ML_D)), spec((1, 1))],
        out_specs=[spec((ML_D, 1)), spec((ML_D, ML_D)), spec((1, ML_D)), spec((1, 1))],
        out_shape=[jax.ShapeDtypeStruct((g_all, ML_D, 1), F32),
                   jax.ShapeDtypeStruct((g_all, ML_D, ML_D), F32),
                   jax.ShapeDtypeStruct((g_all, 1, ML_D), F32),
                   jax.ShapeDtypeStruct((g_all, 1, 1), F32)],
        compiler_params=_cparams(("parallel",)),
        name="mlstm_step",
    )(q, k, vc, li, lfp, c0, n0, m0)


def _mm_res_kernel(a_ref, b_ref, r_ref, o_ref):
    o_ref[...] = r_ref[...] + _dot(a_ref[...], b_ref[...])


def matmul_residual(a, b, res, tm):
    m, kk = a.shape
    n = b.shape[1]
    return pl.pallas_call(
        _mm_res_kernel,
        grid=(m // tm,),
        in_specs=[pl.BlockSpec((tm, kk), lambda i: (i, 0)),
                  pl.BlockSpec((kk, n), lambda i: (0, 0)),
                  pl.BlockSpec((tm, n), lambda i: (i, 0))],
        out_specs=pl.BlockSpec((tm, n), lambda i: (i, 0)),
        out_shape=jax.ShapeDtypeStruct((m, n), F32),
        compiler_params=_cparams(("parallel",)),
        name="merge_matmul",
    )(a, b, res)


def _merge_kernel(cb_ref, cc_ref, ch_ref, mo_ref, hcc_ref, hch_ref, o_ref, hml_ref, cw_ref, gml_ref, w_ref, x_ref,
                  out_ref, *, tiles_per_seq):
    i = pl.program_id(0)
    u = cc_ref[...] * ch_ref[...]
    hu = jnp.where(i % tiles_per_seq == 0, 0.0, hcc_ref[...] * hch_ref[...])
    row = lax.broadcasted_iota(jnp.int32, u.shape, 0)
    prev1 = jnp.where(row == 0, hu[7:8], pltpu.roll(u, 1, 0))
    prev2 = jnp.where(row == 0, hu[6:7], jnp.where(row == 1, hu[7:8], pltpu.roll(u, 2, 0)))
    cw = cw_ref[...]
    conv = cw[0:1] * prev2 + cw[1:2] * prev1 + cw[2:3] * u
    a = (cb_ref[...] * conv).astype(BF16)

    hml = hml_ref[...]
    lane = lax.broadcasted_iota(jnp.int32, hml.shape, 1)
    sq = hml * hml
    scale = jnp.zeros_like(hml)
    for hh in range(ML_HEADS):
        seg = (lane >= hh * ML_D) & (lane < (hh + 1) * ML_D)
        ms = jnp.sum(jnp.where(seg, sq, 0.0), axis=1, keepdims=True) * (1.0 / ML_D)
        scale = jnp.where(seg, lax.rsqrt(ms + RMS_EPS), scale)
    ml = (hml * scale * gml_ref[...] * jax.nn.sigmoid(mo_ref[...])).astype(BF16)

    out_ref[...] = (x_ref[...] + _dot(a, w_ref[0:CONV_W, :])
                    + _dot(o_ref[...].astype(BF16), w_ref[CONV_W:CONV_W + MLA_W, :])
                    + _dot(ml, w_ref[CONV_W + MLA_W:, :]))


def merge_prompt(h, o_mla, h_ml, cw8, g_ml_row, w_out, x, tm, seq):
    m = x.shape[0]
    nb = seq // tm
    mo_blk = (3 * CONV_W + Q_RANK + KV_RANK + 3 * ML_W) // ML_W
    hb = tm // 8

    def col(cb):
        return pl.BlockSpec((tm, CONV_W), lambda i: (i, cb))

    def halo(cb):
        return pl.BlockSpec((8, CONV_W), lambda i: (jnp.maximum(i * hb - 1, 0), cb))

    return pl.pallas_call(
        functools.partial(_merge_kernel, tiles_per_seq=nb),
        grid=(m // tm,),
        in_specs=[col(0), col(1), col(2), col(mo_blk), halo(1), halo(2),
                  pl.BlockSpec((tm, MLA_W), lambda i: (i, 0)),
                  pl.BlockSpec((tm, ML_W), lambda i: (i, 0)),
                  pl.BlockSpec((8, CONV_W), lambda i: (0, 0)),
                  pl.BlockSpec((1, ML_W), lambda i: (0, 0)),
                  pl.BlockSpec((D_MODEL, D_MODEL), lambda i: (0, 0)),
                  pl.BlockSpec((tm, D_MODEL), lambda i: (i, 0))],
        out_specs=pl.BlockSpec((tm, D_MODEL), lambda i: (i, 0)),
        out_shape=jax.ShapeDtypeStruct((m, D_MODEL), F32),
        compiler_params=_cparams(("parallel",)),
        name="merge_prompt",
    )(h, h, h, h, h, h, o_mla, h_ml, cw8, g_ml_row, w_out, x)


def _bmm_kernel(a_ref, b_ref, o_ref):
    o_ref[...] = _dot(a_ref[...], b_ref[...])


def bmm(a, b):
    nb, m, kk = a.shape
    n = b.shape[2]
    return pl.pallas_call(
        _bmm_kernel,
        grid=(nb,),
        in_specs=[pl.BlockSpec((None, m, kk), lambda i: (i, 0, 0)),
                  pl.BlockSpec((None, kk, n), lambda i: (i, 0, 0))],
        out_specs=pl.BlockSpec((None, m, n), lambda i: (i, 0, 0)),
        out_shape=jax.ShapeDtypeStruct((nb, m, n), F32),
        compiler_params=_cparams(("parallel",)),
        name="bmm",
    )(a, b)


def _decode_kernel(pt_ref, ckv_hbm, kr_hbm, qlat_ref, qrope_ref, cnew_ref, rnew_ref, wukt_ref, o_ref,
                   cbuf, rbuf, sem, m_s, l_s, acc_s, *, layer):
    npg = DEC_PAGES
    nch = npg // DEC_CHUNK
    b, j = pl.program_id(0), pl.program_id(1)
    nj = pl.num_programs(1)
    step = b * nj + j
    last_step = pl.num_programs(0) * nj - 1
    slot = step % 2

    def page_copies(at_step, sl):
        bb, jj = at_step // nj, at_step % nj
        cps = []
        for k in range(npg):
            page = pt_ref[bb, jj * npg + k]
            cps.append(pltpu.make_async_copy(ckv_hbm.at[layer, page],
                                             cbuf.at[sl, pl.ds(k * PAGE_SIZE, PAGE_SIZE)], sem.at[0, sl]))
            cps.append(pltpu.make_async_copy(kr_hbm.at[layer, page], rbuf.at[sl, k], sem.at[1, sl]))
        return cps

    @pl.when(step == 0)
    def _():
        for cp in page_copies(step, slot):
            cp.start()

    for cp in page_copies(step, slot):
        cp.wait()
    prefetch = page_copies(jnp.minimum(step + 1, last_step), 1 - slot)
    per_chunk = 2 * len(prefetch) // nch

    @pl.when(j == 0)
    def _():
        m_s[...] = jnp.full_like(m_s, -jnp.inf)
        l_s[...] = jnp.zeros_like(l_s)
        acc_s[...] = jnp.zeros_like(acc_s)

    lhs = jnp.concatenate([wukt_ref[...], qlat_ref[...]], axis=0)
    qrope = qrope_ref[...]
    nk = MLA_HEADS * NOPE_D

    def products(c, rt):
        cb = c.astype(BF16)
        big = _dot_nt(lhs, cb)
        return cb, big, _dot(qrope, rt.astype(BF16)), jnp.sum(rt * rt, axis=0, keepdims=True)

    def weights(prod, valid, m_old):
        cb, big, s_rope, ssq_r = prod
        n = cb.shape[0]
        ssq = jnp.sum(jnp.square(big[:nk].reshape(NOPE_D, MLA_HEADS, n)), axis=0) + ssq_r
        s = (big[nk:nk + MLA_HEADS] + s_rope[:MLA_HEADS]) * lax.rsqrt(ssq * (1.0 / QK_D) + RMS_EPS)
        if valid is not None:
            s = jnp.where(valid, s, -jnp.inf)
        m_new = jnp.maximum(m_old, jnp.max(s, axis=1, keepdims=True))
        return m_new, jnp.exp2(m_old - m_new), jnp.exp2(s - m_new)

    def accumulate(w, cb, l_old, acc):
        _, alpha, p = w
        return alpha * l_old + jnp.sum(p, axis=1, keepdims=True), alpha * acc + _dot(p.astype(BF16), cb)

    rows = DEC_CHUNK * PAGE_SIZE

    def chunk_products(i):
        c = cbuf[slot, pl.ds(i * rows, rows), :]
        rt = jnp.concatenate([rbuf[slot, i * DEC_CHUNK + k] for k in range(DEC_CHUNK)], axis=1)
        return products(c, rt)

    m_run, l_run, acc = m_s[...], l_s[...], acc_s[...]
    prods = [chunk_products(i) for i in range(min(2, nch))]
    for i in range(nch):
        for cp in prefetch[i * per_chunk:(i + 1) * per_chunk]:
            cp.start()
        w = weights(prods[i], None, m_run)
        if i + 2 < nch:
            prods.append(chunk_products(i + 2))
        l_run, acc = accumulate(w, prods[i][0], l_run, acc)
        m_run = w[0]
    m_s[...], l_s[...], acc_s[...] = m_run, l_run, acc

    @pl.when(j == nj - 1)
    def _():
        lane = lax.broadcasted_iota(jnp.int32, (MLA_HEADS, PAGE_SIZE), 1)
        prod = products(cnew_ref[...], rnew_ref[...])
        l_fin, acc_fin = accumulate(weights(prod, lane == 0, m_run), prod[0], l_run, acc)
        o_ref[...] = acc_fin / l_fin

    @pl.when(step == last_step)
    def _():
        for cp in prefetch:
            cp.wait()


def decode_attention(layer, page_table, cache_ckv, cache_krope_t, qlat, qrope, cnew, rnew_t, wukt):
    nseq, n_pages = page_table.shape
    npg = DEC_PAGES
    nj = n_pages // npg

    in_specs = [pl.BlockSpec(memory_space=pl.ANY),
                pl.BlockSpec(memory_space=pl.ANY),
                pl.BlockSpec((None, Q_ROWS, KV_RANK), lambda b, j, pt: (b, 0, 0)),
                pl.BlockSpec((None, Q_ROWS, ROPE_D), lambda b, j, pt: (b, 0, 0)),
                pl.BlockSpec((None, PAGE_SIZE, KV_RANK), lambda b, j, pt: (b, 0, 0)),
                pl.BlockSpec((None, ROPE_D, PAGE_SIZE), lambda b, j, pt: (b, 0, 0)),
                pl.BlockSpec((MLA_HEADS * NOPE_D, KV_RANK), lambda b, j, pt: (0, 0))]
    return pl.pallas_call(
        functools.partial(_decode_kernel, layer=layer),
        grid_spec=pltpu.PrefetchScalarGridSpec(
            num_scalar_prefetch=1,
            grid=(nseq, nj),
            in_specs=in_specs,
            out_specs=pl.BlockSpec((None, MLA_HEADS, KV_RANK), lambda b, j, pt: (b, 0, 0)),
            scratch_shapes=[pltpu.VMEM((2, npg * PAGE_SIZE, KV_RANK), F32),
                            pltpu.VMEM((2, npg, ROPE_D, PAGE_SIZE), F32),
                            pltpu.SemaphoreType.DMA((2, 2)),
                            pltpu.VMEM((MLA_HEADS, 1), F32),
                            pltpu.VMEM((MLA_HEADS, 1), F32),
                            pltpu.VMEM((MLA_HEADS, KV_RANK), F32)]),
        out_shape=jax.ShapeDtypeStruct((nseq, MLA_HEADS, KV_RANK), F32),
        compiler_params=_cparams(("arbitrary", "arbitrary")),
        name="decode_attention",
    )(page_table, cache_ckv, cache_krope_t, qlat, qrope, cnew, rnew_t, wukt)


def _ffn_kernel(x_ref, g_ref, w1_ref, w3_ref, w2_ref, o_ref, xn_ref):
    f = pl.program_id(1)

    @pl.when(f == 0)
    def _():
        x = x_ref[...]
        xn_ref[...] = _rms(x, g_ref[...]).astype(BF16)
        o_ref[...] = x

    xn = xn_ref[...]
    a = _dot(xn, w1_ref[...])
    b = _dot(xn, w3_ref[...])
    act = (a * jax.nn.sigmoid(a) * b).astype(BF16)
    o_ref[...] += _dot(act, w2_ref[...])


def ffn_dense(x, g, w1, w3, w2, tm, tf):
    m = x.shape[0]
    nf = w1.shape[1] // tf
    return pl.pallas_call(
        _ffn_kernel,
        grid=(m // tm, nf),
        in_specs=[pl.BlockSpec((tm, D_MODEL), lambda i, f: (i, 0)),
                  pl.BlockSpec((1, D_MODEL), lambda i, f: (0, 0)),
                  pl.BlockSpec((D_MODEL, tf), lambda i, f: (0, f)),
                  pl.BlockSpec((D_MODEL, tf), lambda i, f: (0, f)),
                  pl.BlockSpec((tf, D_MODEL), lambda i, f: (f, 0))],
        out_specs=pl.BlockSpec((tm, D_MODEL), lambda i, f: (i, 0)),
        out_shape=jax.ShapeDtypeStruct((m, D_MODEL), F32),
        scratch_shapes=[pltpu.VMEM((tm, D_MODEL), BF16)],
        compiler_params=_cparams(("parallel", "arbitrary")),
        name="ffn_dense",
    )(x, g, w1, w3, w2)


def _router_kernel(x_ref, g_ref, wh_ref, wl_ref, info_ref):
    xn = _rms(x_ref[...], g_ref[...])
    hi, lo = _split2(xn)
    lg =_dot(hi, wh_ref[...]) + _dot(hi, wl_ref[...]) + _dot(lo, wh_ref[...])
    lane = lax.broadcasted_iota(jnp.int32, lg.shape, 1).astype(F32)
    lg = jnp.where(lane < N_EXPERTS, lg, -jnp.inf)
    v1 = jnp.max(lg, axis=1, keepdims=True)
    i1 = jnp.min(jnp.where(lg == v1, lane, float(LANE)), axis=1, keepdims=True)
    rest = jnp.where(lane == i1, -jnp.inf, lg)
    v2 = jnp.max(rest, axis=1, keepdims=True)
    i2 = jnp.min(jnp.where(rest == v2, lane, float(LANE)), axis=1, keepdims=True)
    e = jnp.exp(v2 - v1)
    g1 = 1.0 / (1.0 + e)
    info_ref[...] = jnp.where(lane == 0, i1, jnp.where(lane == 1, i2, jnp.where(lane == 2, g1, e * g1)))


def router(x, g, wr_hi, wr_lo, tm):
    m = x.shape[0]
    return pl.pallas_call(
        _router_kernel,
        grid=(m // tm,),
        in_specs=[pl.BlockSpec((tm, D_MODEL), lambda i: (i, 0)),
                  pl.BlockSpec((1, D_MODEL), lambda i: (0, 0)),
                  pl.BlockSpec((D_MODEL, LANE), lambda i: (0, 0)),
                  pl.BlockSpec((D_MODEL, LANE), lambda i: (0, 0))],
        out_specs=pl.BlockSpec((tm, LANE), lambda i: (i, 0)),
        out_shape=jax.ShapeDtypeStruct((m, LANE), F32),
        compiler_params=_cparams(("parallel",)),
        name="router",
    )(x, g, wr_hi, wr_lo)


def _moe_block_kernel(be_ref, nu_ref, x_ref, g_ref, gate_ref, w1_ref, w3_ref, w2_ref, o_ref):
    del be_ref
    i = pl.program_id(0)

    @pl.when(i < nu_ref[0])
    def _():
        x = _rms(x_ref[...], g_ref[...]).astype(BF16)
        a = _dot(x, w1_ref[...])
        b = _dot(x, w3_ref[...])
        act = (a * jax.nn.sigmoid(a) * b).astype(BF16)
        o_ref[...] = _dot(act, w2_ref[...]) * gate_ref[...]

    @pl.when(i >= nu_ref[0])
    def _():
        o_ref[...] = jnp.zeros_like(o_ref)


def moe_blocks(blk_exp, n_used, xg, g, slot_gate, w1, w3, w2):
    n_slot = xg.shape[0]
    tm = MOE_TM
    return pl.pallas_call(
        _moe_block_kernel,
        grid_spec=pltpu.PrefetchScalarGridSpec(
            num_scalar_prefetch=2,
            grid=(n_slot // tm,),
            in_specs=[pl.BlockSpec((tm, D_MODEL), lambda i, be, nu: (i, 0)),
                      pl.BlockSpec((1, D_MODEL), lambda i, be, nu: (0, 0)),
                      pl.BlockSpec((tm, 1), lambda i, be, nu: (i, 0)),
                      pl.BlockSpec((None, D_MODEL, D_FF_EXPERT), lambda i, be, nu: (be[i], 0, 0)),
                      pl.BlockSpec((None, D_MODEL, D_FF_EXPERT), lambda i, be, nu: (be[i], 0, 0)),
                      pl.BlockSpec((None, D_FF_EXPERT, D_MODEL), lambda i, be, nu: (be[i], 0, 0))],
            out_specs=pl.BlockSpec((tm, D_MODEL), lambda i, be, nu: (i, 0))),
        out_shape=jax.ShapeDtypeStruct((n_slot, D_MODEL), F32),
        compiler_params=_cparams(("arbitrary",)),
        name="moe_blocks",
    )(blk_exp, n_used, xg, g, slot_gate, w1, w3, w2)


def _add3_kernel(a_ref, b_ref, c_ref, o_ref):
    o_ref[...] = a_ref[...] + b_ref[...] + c_ref[...]


def add3(a, b, c, tm):
    m, n = a.shape
    spec = pl.BlockSpec((tm, n), lambda i: (i, 0))
    return pl.pallas_call(
        _add3_kernel,
        grid=(m // tm,),
        in_specs=[spec, spec, spec],
        out_specs=spec,
        out_shape=jax.ShapeDtypeStruct((m, n), F32),
        compiler_params=_cparams(("parallel",)),
        name="moe_combine",
    )(a, b, c)


def _moe_dense_kernel(x_ref, g_ref, gate_ref, w1_ref, w3_ref, w2_ref, o_ref):
    e = pl.program_id(0)

    @pl.when(e == 0)
    def _():
        o_ref[...] = x_ref[...]

    xn = _rms(x_ref[...], g_ref[...]).astype(BF16)
    a = _dot(xn, w1_ref[...])
    b = _dot(xn, w3_ref[...])
    act = (a * jax.nn.sigmoid(a) * b).astype(BF16)
    o_ref[...] += _dot(act, w2_ref[...]) * gate_ref[...]


def moe_dense(x, g, gates, w1, w3, w2):
    m = x.shape[0]
    return pl.pallas_call(
        _moe_dense_kernel,
        grid=(N_EXPERTS,),
        in_specs=[pl.BlockSpec((m, D_MODEL), lambda e: (0, 0)),
                  pl.BlockSpec((1, D_MODEL), lambda e: (0, 0)),
                  pl.BlockSpec((None, m, 1), lambda e: (e, 0, 0)),
                  pl.BlockSpec((None, D_MODEL, D_FF_EXPERT), lambda e: (e, 0, 0)),
                  pl.BlockSpec((None, D_MODEL, D_FF_EXPERT), lambda e: (e, 0, 0)),
                  pl.BlockSpec((None, D_FF_EXPERT, D_MODEL), lambda e: (e, 0, 0))],
        out_specs=pl.BlockSpec((m, D_MODEL), lambda e: (0, 0)),
        out_shape=jax.ShapeDtypeStruct((m, D_MODEL), F32),
        compiler_params=_cparams(("arbitrary",)),
        name="moe_dense",
    )(x, g, gates, w1, w3, w2)


def _rope_tables(pos, rope_lane0):
    inv_freq = ROPE_THETA ** (-jnp.arange(HALF_ROPE, dtype=F32) / HALF_ROPE)
    ang = pos.astype(F32)[:, None] * inv_freq[None, :]
    cos, sin = jnp.cos(ang), jnp.sin(ang)
    n = pos.shape[0]
    z = lambda w: jnp.zeros((n, w), F32)
    tail = LANE - rope_lane0 - ROPE_D
    c = jnp.concatenate([jnp.ones((n, rope_lane0), F32), cos, cos, z(tail)], axis=1)
    s1 = jnp.concatenate([z(rope_lane0), -sin, z(HALF_ROPE), z(tail)], axis=1)
    s2 = jnp.concatenate([z(rope_lane0), z(HALF_ROPE), sin, z(tail)], axis=1)
    return c, s1, s2


def _head_gain(g):
    return jnp.concatenate([g[:NOPE_D], g[NOPE_D:], g[NOPE_D:], jnp.zeros((HP - QK_D,), F32)])[None, :]


def _prep_layer(l, w_in, g_cq, w_uq, g_ckv, w_uk, w_uv, g_q, g_k, b_i, b_f, w_out):
    d = {}
    wi = w_in[l]
    kr0 = 3 * CONV_W + Q_RANK + KV_RANK
    d["w_in"] = jnp.concatenate(
        [wi[:, :kr0], wi[:, kr0 + ROPE_D:kr0 + ROPE_D + 4 * ML_W], wi[:, kr0:kr0 + ROPE_D],
         wi[:, kr0 + ROPE_D + 4 * ML_W:], jnp.zeros((D_MODEL, LANE - ROPE_D - 2 * ML_HEADS), F32)],
        axis=1).astype(BF16)
    wq = w_uq[l].reshape(Q_RANK, MLA_HEADS, QK_D)
    d["w_uq"] = jnp.pad(wq, ((0, 0), (0, 0), (0, HP - QK_D))).reshape(Q_RANK, MLA_HEADS * HP).astype(BF16)
    wk = jnp.pad(w_uk[l], ((0, 0), (0, 0), (0, HP - NOPE_D))).reshape(KV_RANK, MLA_HEADS * HP)
    place = jnp.zeros((LANE, MLA_HEADS, HP), F32)
    place = place.at[jnp.arange(ROPE_D)[:, None], jnp.arange(MLA_HEADS)[None, :],
                     NOPE_D + jnp.arange(ROPE_D)[:, None]].set(1.0)
    d["w_k"] = jnp.concatenate([wk, place.reshape(LANE, MLA_HEADS * HP)], axis=0).astype(BF16)
    d["w_vt"] = w_uv[l].reshape(KV_RANK, MLA_W).T.astype(BF16)
    d["w_ukt_dec"] = w_uk[l].transpose(2, 1, 0).reshape(NOPE_D * MLA_HEADS, KV_RANK).astype(BF16)
    d["w_uk_h"] = w_uk[l].transpose(1, 2, 0).astype(BF16)
    d["w_uv_h"] = w_uv[l].transpose(1, 0, 2).astype(BF16)
    d["g_cq"] = g_cq[l][None, :]
    d["g_ckv"] = g_ckv[l][None, :]
    d["g_q"] = _head_gain(g_q[l])
    d["g_k"] = _head_gain(g_k[l])
    bias = jnp.zeros((LANE,), F32).at[GATE_I:GATE_I + ML_HEADS].set(b_i[l])
    d["gate_bias"] = bias.at[GATE_F:GATE_F + ML_HEADS].set(b_f[l])[None, :]
    d["w_out"] = w_out[l].astype(BF16)
    return d


def _split_h(h):
    o = 0
    out = []
    for w in (CONV_W, CONV_W, CONV_W, Q_RANK, KV_RANK, ML_W, ML_W, ML_W, ML_W):
        out.append(h[:, o:o + w])
        o += w
    return out


def _ml_merge(h_ml, mo, g_ml_l):
    hm = h_ml.reshape(-1, ML_HEADS, ML_D)
    ml = hm * lax.rsqrt(jnp.mean(hm * hm, axis=-1, keepdims=True) + RMS_EPS) * g_ml_l[None]
    return ml.reshape(-1, ML_W) * jax.nn.sigmoid(mo)


def _route(info):
    return info[:, TOP_K:2 * TOP_K], info[:, :TOP_K].astype(jnp.int32)


def kernel(x_prompt, x_sample, cache_ckv, cache_krope, page_table, state_conv, state_C, state_n, state_m,
           g_norm_mix, g_norm_ffn, w_in, conv_w, g_cq, w_uq, g_ckv, w_uk, w_uv, g_q, g_k, b_i, b_f, g_ml,
           w_out, w_ff1, w_ff3, w_ff2, w_router, w_e1, w_e3, w_e2):
    batch, seq, _ = x_prompt.shape
    nseq = x_sample.shape[0]
    depth = w_in.shape[0]
    past_len = page_table.shape[1] * PAGE_SIZE
    mp = batch * seq
    tm_p = 512

    xp = x_prompt.reshape(mp, D_MODEL)
    xs = x_sample.reshape(nseq, D_MODEL)

    pos_p = jnp.arange(seq)
    pos_s = jnp.full((nseq,), past_len)
    tabs_q_p = _rope_tables(pos_p, NOPE_D)
    tabs_k_p = _rope_tables(pos_p, 0)
    tabs_q_s = _rope_tables(pos_s, NOPE_D)
    tabs_k_s = _rope_tables(pos_s, 0)
    tri = jnp.tril(jnp.ones((ML_L, ML_L), F32)).astype(BF16)
    scale_row = jnp.full((1, HP), ATTN_SCALE * LOG2E, F32)
    cache_krope_t = cache_krope.transpose(0, 1, 3, 2)

    outs = {k: [] for k in ("ckv_p", "ckv_s", "kr_p", "kr_s", "cv_p", "cv_s",
                            "C_p", "C_s", "n_p", "n_s", "m_p", "m_s")}

    for l in range(depth):
        p = _prep_layer(l, w_in, g_cq, w_uq, g_ckv, w_uk, w_uv, g_q, g_k, b_i, b_f, w_out)
        gmix = g_norm_mix[l][None, :]
        gffn = g_norm_ffn[l][None, :]
        cw = conv_w[l]

        h = inproj(xp, gmix, p["w_in"], tm_p)
        q = q_path(h, p["g_cq"], p["w_uq"], p["g_q"] * scale_row, tabs_q_p, tm_p, seq // tm_p)
        ckvn, krr = ckv_path(h, p["g_ckv"], tabs_k_p, tm_p, seq // tm_p)
        k, vt = kv_path(ckvn, krr, p["w_k"], p["w_vt"], p["g_k"], batch, seq)
        o_mla = prompt_attention(q.reshape(batch, seq, -1), k.reshape(batch, seq, -1), vt, batch, seq)
        h_ml, c1, n1, m1 = mlstm_prompt(h, p["gate_bias"], tri, batch, seq)

        cw8 = jnp.pad(cw, ((0, 8 - CONV_K), (0, 0)))
        xp = merge_prompt(h, o_mla.reshape(mp, MLA_W), h_ml, cw8, g_ml[l].reshape(1, ML_W), p["w_out"], xp,
                          tm_p, seq)
        tail = h.reshape(batch, seq, IN_PAD)[:, seq - (CONV_K - 1):, CONV_W:3 * CONV_W]

        outs["ckv_p"].append(ckvn.reshape(batch, seq, KV_RANK))
        outs["kr_p"].append(krr[:, :ROPE_D].reshape(batch, seq, ROPE_D))
        outs["cv_p"].append(tail[..., :CONV_W] * tail[..., CONV_W:])
        outs["C_p"].append(c1)
        outs["n_p"].append(n1.reshape(batch, ML_HEADS, ML_D))
        outs["m_p"].append(m1.reshape(batch, ML_HEADS))

        hs = inproj(xs, gmix, p["w_in"], nseq)
        cb, cc, ch, _, _, mq, mk, mv, mo = _split_h(hs)
        gates = hs[:, IN_MAIN:]
        qs = q_path(hs, p["g_cq"], p["w_uq"], p["g_q"] * p["g_k"] * scale_row, tabs_q_s, nseq, 1)
        ckvn_s, krr_s = ckv_path(hs, p["g_ckv"], tabs_k_s, nseq, 1)
        qs3 = qs.reshape(nseq, MLA_HEADS, HP)
        qlat = bmm(qs3[:, :, :NOPE_D].transpose(1, 0, 2), p["w_uk_h"])
        qpad = ((0, 0), (0, Q_ROWS - MLA_HEADS), (0, 0))
        qlat = jnp.pad(qlat.transpose(1, 0, 2).astype(BF16), qpad)
        qrope = jnp.pad(qs3[:, :, NOPE_D:QK_D], qpad)
        cnew = jnp.pad(ckvn_s[:, None, :], ((0, 0), (0, PAGE_SIZE - 1), (0, 0)))
        rnew_t = jnp.pad(krr_s[:, :ROPE_D, None], ((0, 0), (0, 0), (0, PAGE_SIZE - 1)))
        pc = decode_attention(l, page_table, cache_ckv, cache_krope_t, qlat, qrope, cnew, rnew_t, p["w_ukt_dec"])
        o_s = bmm(pc.transpose(1, 0, 2).astype(BF16), p["w_uv_h"])
        o_s = o_s.transpose(1, 0, 2).reshape(nseq, MLA_W)

        g_all = nseq * ML_HEADS
        li = (gates[:, GATE_I:GATE_I + ML_HEADS] + b_i[l][None, :]).reshape(g_all, 1, 1)
        lfp = (gates[:, GATE_F:GATE_F + ML_HEADS] + b_f[l][None, :]).reshape(g_all, 1, 1)
        h_col, c1s, n1s, m1s = mlstm_step(
            mq.reshape(g_all, 1, ML_D), mk.reshape(g_all, 1, ML_D), mv.reshape(g_all, ML_D, 1), li, lfp,
            state_C[l].reshape(g_all, ML_D, ML_D), state_n[l].reshape(g_all, 1, ML_D),
            state_m[l].reshape(g_all, 1, 1))

        us = cc * ch
        st = state_conv[l]
        conv_s = cw[0] * st[:, 0] + cw[1] * st[:, 1] + cw[2] * us
        mix = jnp.concatenate([cb * conv_s, o_s, _ml_merge(h_col.reshape(nseq, ML_W), mo, g_ml[l])], axis=1)
        xs = matmul_residual(mix.astype(BF16), p["w_out"], xs, nseq)

        outs["ckv_s"].append(ckvn_s[:, None, :])
        outs["kr_s"].append(krr_s[:, None, :ROPE_D])
        outs["cv_s"].append(jnp.stack([st[:, 1], us], axis=1))
        outs["C_s"].append(c1s.reshape(nseq, ML_HEADS, ML_D, ML_D))
        outs["n_s"].append(n1s.reshape(nseq, ML_HEADS, ML_D))
        outs["m_s"].append(m1s.reshape(nseq, ML_HEADS))

        j = l // 2
        if l % 2 == 0:
            w1, w3, w2 = w_ff1[j].astype(BF16), w_ff3[j].astype(BF16), w_ff2[j].astype(BF16)
            xp = ffn_dense(xp, gffn, w1, w3, w2, tm_p, D_FF // 2)
            xs = ffn_dense(xs, gffn, w1, w3, w2, nseq, D_FF // 2)
        else:
            w1, w3, w2 = w_e1[j].astype(BF16), w_e3[j].astype(BF16), w_e2[j].astype(BF16)
            wr = jnp.pad(w_router[j], ((0, 0), (0, LANE - N_EXPERTS)))
            wr_hi = wr.astype(BF16)
            wr_lo = (wr - wr_hi.astype(F32)).astype(BF16)

            gate, top_i = _route(router(xp, gffn, wr_hi, wr_lo, tm_p))
            e_flat = top_i.reshape(-1)
            n_assign = mp * TOP_K
            onehot = (e_flat[:, None] == jnp.arange(N_EXPERTS)[None, :]).astype(jnp.int32)
            rank = jnp.take_along_axis(jnp.cumsum(onehot, axis=0) - onehot, e_flat[:, None], axis=1)[:, 0]
            counts = jnp.sum(onehot, axis=0)
            padded = (counts + MOE_TM - 1) // MOE_TM * MOE_TM
            pad_end = jnp.cumsum(padded)
            pad_start = pad_end - padded
            dest = pad_start[e_flat] + rank
            n_blk = (n_assign + N_EXPERTS * (MOE_TM - 1) + MOE_TM - 1) // MOE_TM
            n_slot = n_blk * MOE_TM
            slot_asg = jnp.full((n_slot,), -1, jnp.int32).at[dest].set(jnp.arange(n_assign, dtype=jnp.int32))
            slot_tok = jnp.maximum(slot_asg, 0) // TOP_K
            slot_gate = jnp.where(slot_asg >= 0, gate.reshape(-1)[jnp.maximum(slot_asg, 0)], 0.0)
            blk_start = jnp.arange(n_blk, dtype=jnp.int32) * MOE_TM
            blk_exp = jnp.minimum(jnp.sum((pad_end[None, :] <= blk_start[:, None]).astype(jnp.int32), axis=1),
                                  N_EXPERTS - 1)
            n_used = (pad_end[-1:] // MOE_TM).astype(jnp.int32)
            yb = moe_blocks(blk_exp, n_used, xp[slot_tok], gffn, slot_gate[:, None], w1, w3, w2)
            dest2 = dest.reshape(mp, TOP_K)
            xp = add3(xp, yb[dest2[:, 0]], yb[dest2[:, 1]], tm_p)

            gate_s, top_s = _route(router(xs, gffn, wr_hi, wr_lo, nseq))
            dense_gate = jnp.sum(
                (top_s[:, :, None] == jnp.arange(N_EXPERTS)[None, None, :]) * gate_s[:, :, None], axis=1)
            xs = moe_dense(xs, gffn, dense_gate.T[:, :, None], w1, w3, w2)

    return (xp.reshape(batch, seq, D_MODEL), xs.reshape(nseq, 1, D_MODEL),
            jnp.stack(outs["ckv_p"]), jnp.stack(outs["ckv_s"]),
            jnp.stack(outs["kr_p"]), jnp.stack(outs["kr_s"]),
            jnp.stack(outs["cv_p"]), jnp.stack(outs["cv_s"]),
            jnp.stack(outs["C_p"]), jnp.stack(outs["C_s"]),
            jnp.stack(outs["n_p"]), jnp.stack(outs["n_s"]),
            jnp.stack(outs["m_p"]), jnp.stack(outs["m_s"]))
```

```python
import functools
import math

import jax
import jax.numpy as jnp
import numpy as np
from jax import lax
from jax.experimental import pallas as pl
from jax.experimental.pallas import tpu as pltpu

F32 = jnp.float32
BF16 = jnp.bfloat16

D_MODEL = 1024
PAGE_SIZE = 128
CONV_W = 256
CONV_K = 3
MLA_HEADS = 8
NOPE_D = 64
ROPE_D = 32
HALF_ROPE = ROPE_D // 2
QK_D = NOPE_D + ROPE_D
V_D = 64
MLA_W = MLA_HEADS * V_D
Q_RANK = 384
KV_RANK = 128
ROPE_THETA = 10000.0
ATTN_SCALE = QK_D ** -0.5
ML_HEADS = 4
ML_D = 64
ML_W = ML_HEADS * ML_D
D_FF = 2816
N_EXPERTS = 8
TOP_K = 2
D_FF_EXPERT = 1408
RMS_EPS = 1e-6
LOG2E = math.log2(math.e)

LANE = 128
HP = LANE
IN_MAIN = 2304
IN_PAD = IN_MAIN + LANE
GATE_I = ROPE_D
GATE_F = ROPE_D + ML_HEADS
VMEM_LIMIT = 56 * 1024 * 1024

ATTN_T = 512
ATTN_HG = 2
ATTN_SUM_ROWS = 16
ATTN_CK = 256
ML_L = 128
DEC_PAGES = 128
DEC_CHUNK = 16
Q_ROWS = 16
MOE_TM = 512


def _cparams(sem):
    return pltpu.CompilerParams(dimension_semantics=sem, vmem_limit_bytes=VMEM_LIMIT)


def _rms(x, g):
    return x * lax.rsqrt(jnp.mean(x * x, axis=-1, keepdims=True) + RMS_EPS) * g


def _dot(a, b):
    return jnp.dot(a, b, preferred_element_type=F32)


def _dot_nt(a, b):
    return lax.dot_general(a, b, (((1,), (1,)), ((), ())), preferred_element_type=F32)


def _split2(x):
    hi = x.astype(BF16)
    lo = (x - hi.astype(F32)).astype(BF16)
    return hi, lo


def _split3(x):
    hi = x.astype(BF16)
    r = x - hi.astype(F32)
    mid = r.astype(BF16)
    lo = (r - mid.astype(F32)).astype(BF16)
    return hi, mid, lo


def _inproj_kernel(x_ref, g_ref, w_ref, o_ref):
    xn = _rms(x_ref[...], g_ref[...]).astype(BF16)
    o_ref[...] = _dot(xn, w_ref[...])


def inproj(x, g, w, tm):
    m = x.shape[0]
    n = w.shape[1]
    return pl.pallas_call(
        _inproj_kernel,
        grid=(m // tm,),
        in_specs=[pl.BlockSpec((tm, D_MODEL), lambda i: (i, 0)),
                  pl.BlockSpec((1, D_MODEL), lambda i: (0, 0)),
                  pl.BlockSpec((D_MODEL, n), lambda i: (0, 0))],
        out_specs=pl.BlockSpec((tm, n), lambda i: (i, 0)),
        out_shape=jax.ShapeDtypeStruct((m, n), F32),
        compiler_params=_cparams(("parallel",)),
        name="inproj",
    )(x, g, w)


def _q_kernel(cq_ref, gcq_ref, wuq_ref, gq_ref, c_ref, s1_ref, s2_ref, o_ref):
    cqn = _rms(cq_ref[...], gcq_ref[...]).astype(BF16)
    q = _dot(cqn, wuq_ref[...])
    c, s1, s2, gq = c_ref[...], s1_ref[...], s2_ref[...], gq_ref[...]
    for h in range(MLA_HEADS):
        qh = q[:, h * HP:(h + 1) * HP]
        qh = qh * c + pltpu.roll(qh, HP - HALF_ROPE, 1) * s1 + pltpu.roll(qh, HALF_ROPE, 1) * s2
        ssq = jnp.sum(qh * qh, axis=-1, keepdims=True)
        qh = qh * lax.rsqrt(ssq * (1.0 / QK_D) + RMS_EPS) * gq
        o_ref[:, h * HP:(h + 1) * HP] = qh.astype(o_ref.dtype)


def q_path(h, gcq, wuq, gq, tabs, tm, n_pos_blk):
    m = h.shape[0]
    cq_blk = (3 * CONV_W) // Q_RANK
    tab_spec = pl.BlockSpec((tm, HP), lambda i: (i % n_pos_blk, 0))
    return pl.pallas_call(
        _q_kernel,
        grid=(m // tm,),
        in_specs=[pl.BlockSpec((tm, Q_RANK), lambda i: (i, cq_blk)),
                  pl.BlockSpec((1, Q_RANK), lambda i: (0, 0)),
                  pl.BlockSpec((Q_RANK, MLA_HEADS * HP), lambda i: (0, 0)),
                  pl.BlockSpec((1, HP), lambda i: (0, 0)),
                  tab_spec, tab_spec, tab_spec],
        out_specs=pl.BlockSpec((tm, MLA_HEADS * HP), lambda i: (i, 0)),
        out_shape=jax.ShapeDtypeStruct((m, MLA_HEADS * HP), BF16),
        compiler_params=_cparams(("parallel",)),
        name="q_path",
    )(h, gcq, wuq, gq, *tabs)


def _ckv_kernel(ckv_ref, kr_ref, g_ref, c_ref, s1_ref, s2_ref, ckvn_ref, krr_ref):
    ckvn_ref[...] = _rms(ckv_ref[...], g_ref[...])
    kr = kr_ref[...]
    krr_ref[...] = (kr * c_ref[...] + pltpu.roll(kr, LANE - HALF_ROPE, 1) * s1_ref[...]
                    + pltpu.roll(kr, HALF_ROPE, 1) * s2_ref[...])


def ckv_path(h, g_ckv, tabs, tm, n_pos_blk):
    m = h.shape[0]
    ckv_blk = (3 * CONV_W + Q_RANK) // KV_RANK
    kr_blk = IN_MAIN // LANE
    tab_spec = pl.BlockSpec((tm, LANE), lambda i: (i % n_pos_blk, 0))
    return pl.pallas_call(
        _ckv_kernel,
        grid=(m // tm,),
        in_specs=[pl.BlockSpec((tm, KV_RANK), lambda i: (i, ckv_blk)),
                  pl.BlockSpec((tm, LANE), lambda i: (i, kr_blk)),
                  pl.BlockSpec((1, KV_RANK), lambda i: (0, 0)),
                  tab_spec, tab_spec, tab_spec],
        out_specs=[pl.BlockSpec((tm, KV_RANK), lambda i: (i, 0)),
                   pl.BlockSpec((tm, LANE), lambda i: (i, 0))],
        out_shape=[jax.ShapeDtypeStruct((m, KV_RANK), F32),
                   jax.ShapeDtypeStruct((m, LANE), F32)],
        compiler_params=_cparams(("parallel",)),
        name="ckv_path",
    )(h, h, g_ckv, *tabs)


def _kv_kernel(ckvn_ref, krr_ref, wk_ref, wvt_ref, gk_ref, k_ref, vt_ref):
    cb = ckvn_ref[...].astype(BF16)
    ck = jnp.concatenate([cb, krr_ref[...].astype(BF16)], axis=1)
    k = _dot(ck, wk_ref[...])
    gk = gk_ref[...]
    for h in range(MLA_HEADS):
        kh = k[:, h * HP:(h + 1) * HP]
        ssq = jnp.sum(kh * kh, axis=-1, keepdims=True)
        k_ref[:, h * HP:(h + 1) * HP] = (kh * lax.rsqrt(ssq * (1.0 / QK_D) + RMS_EPS) * gk).astype(BF16)
    vt_ref[...] = _dot_nt(wvt_ref[...], cb).astype(BF16)


def kv_path(ckvn, krr, wk, wvt, gk, batch, seq):
    tm = ATTN_T
    m = ckvn.shape[0]
    nb = seq // tm
    return pl.pallas_call(
        _kv_kernel,
        grid=(m // tm,),
        in_specs=[pl.BlockSpec((tm, KV_RANK), lambda i: (i, 0)),
                  pl.BlockSpec((tm, LANE), lambda i: (i, 0)),
                  pl.BlockSpec((2 * LANE, MLA_HEADS * HP), lambda i: (0, 0)),
                  pl.BlockSpec((MLA_W, KV_RANK), lambda i: (0, 0)),
                  pl.BlockSpec((1, HP), lambda i: (0, 0))],
        out_specs=[pl.BlockSpec((tm, MLA_HEADS * HP), lambda i: (i, 0)),
                   pl.BlockSpec((None, None, MLA_W, tm), lambda i: (i // nb, i % nb, 0, 0))],
        out_shape=[jax.ShapeDtypeStruct((m, MLA_HEADS * HP), BF16),
                   jax.ShapeDtypeStruct((batch, nb, MLA_W, tm), BF16)],
        compiler_params=_cparams(("parallel",)),
        name="kv_path",
    )(ckvn, krr, wk, wvt, gk)


def _attn_kernel(q_ref, k_ref, vt_ref, o_ref, s0_ref, s1_ref):
    t = ATTN_T
    hg = ATTN_HG
    qi = pl.program_id(2)
    row = lax.broadcasted_iota(jnp.int32, (t, t), 0)
    col = lax.broadcasted_iota(jnp.int32, (t, t), 1)
    causal = row <= col
    ones_rows = jnp.ones((ATTN_SUM_ROWS, t), BF16)

    def qk(j, dst, masked):
        cms = []
        for hh in range(hg):
            k = k_ref[pl.ds(pl.multiple_of(j * t, t), t), hh * HP:(hh + 1) * HP]
            st = _dot_nt(k, q_ref[:, hh * HP:(hh + 1) * HP])
            st = jnp.where(causal, st, -jnp.inf) if masked else st
            dst[hh] = st
            cms.append(jnp.max(st, axis=0, keepdims=True))
        return tuple(cms)

    def process(j, src, cms, carry):
        out = []
        for hh in range(hg):
            m, acc = carry[2 * hh:2 * hh + 2]
            m_new = jnp.maximum(m, cms[hh])
            p = jnp.exp2(src[hh] - m_new).astype(BF16)
            alpha = jnp.exp2(m - m_new)
            va = jnp.concatenate([vt_ref[j, hh * V_D:(hh + 1) * V_D, :], ones_rows], axis=0)
            out += [m_new, alpha * acc + _dot(va, p)]
        return tuple(out)

    def fused(jq, dst, jp, src, cms, carry):
        ck = ATTN_CK
        out, new_cms = [], []
        for hh in range(hg):
            m, acc = carry[2 * hh:2 * hh + 2]
            m_new = jnp.maximum(m, cms[hh])
            acc = jnp.exp2(m - m_new) * acc
            k = k_ref[pl.ds(pl.multiple_of(jq * t, t), t), hh * HP:(hh + 1) * HP]
            st = _dot_nt(k, q_ref[:, hh * HP:(hh + 1) * HP])
            dst[hh] = st
            cm = jnp.max(st, axis=0, keepdims=True)
            for c in range(t // ck):
                p = jnp.exp2(src[hh, c * ck:(c + 1) * ck, :] - m_new).astype(BF16)
                va = jnp.concatenate([vt_ref[jp, hh * V_D:(hh + 1) * V_D, c * ck:(c + 1) * ck],
                                      ones_rows[:, :ck]], axis=0)
                acc = acc + _dot(va, p)
            new_cms.append(cm)
            out += [m_new, acc]
        return tuple(out), tuple(new_cms)

    ns = 2 * hg
    cm0 = qk(qi, s0_ref, True)
    init = (jnp.full((1, t), -jnp.inf, F32), jnp.zeros((V_D + ATTN_SUM_ROWS, t), F32)) * hg + cm0

    def body(tt, carry):
        state, cm1 = fused(2 * tt, s1_ref, jnp.where(tt == 0, qi, 2 * tt - 1), s0_ref, carry[ns:], carry[:ns])
        state, cm0n = fused(jnp.minimum(2 * tt + 1, jnp.maximum(qi - 1, 0)), s0_ref, 2 * tt, s1_ref, cm1, state)
        return state + cm0n

    carry = lax.fori_loop(0, (qi + 1) // 2, body, init)
    carry = lax.cond(qi % 2 == 0,
                     lambda c: process(jnp.where(qi == 0, 0, qi - 1), s0_ref, c[ns:], c[:ns]),
                     lambda c: c[:ns], carry)
    o_t = jnp.concatenate([carry[2 * hh + 1][:V_D] / carry[2 * hh + 1][V_D:V_D + 1] for hh in range(hg)], axis=0)
    o_ref[...] = o_t.T


def prompt_attention(q, k, vt, batch, seq):
    t = ATTN_T
    hg = ATTN_HG
    nb = seq // t
    return pl.pallas_call(
        _attn_kernel,
        grid=(batch, MLA_HEADS // hg, nb),
        in_specs=[pl.BlockSpec((None, t, hg * HP), lambda b, p, i: (b, i, p)),
                  pl.BlockSpec((None, seq, hg * HP), lambda b, p, i: (b, 0, p)),
                  pl.BlockSpec((None, nb, hg * V_D, t), lambda b, p, i: (b, 0, p, 0))],
        out_specs=pl.BlockSpec((None, t, hg * V_D), lambda b, p, i: (b, i, p)),
        out_shape=jax.ShapeDtypeStruct((batch, seq, MLA_W), F32),
        scratch_shapes=[pltpu.VMEM((hg, t, t), F32), pltpu.VMEM((hg, t, t), F32)],
        compiler_params=_cparams(("parallel", "parallel", "arbitrary")),
        name="prompt_attention",
    )(q, k, vt)


def _log_sigmoid(x):
    return jnp.minimum(x, 0.0) - jnp.log1p(jnp.exp(-jnp.abs(x)))


def _mlstm_kernel(q_ref, k_ref, v_ref, g_ref, bias_ref, tri_ref,
                  h_ref, c_out, n_out, m_out, c_s, n_s, m_s):
    L = ML_L
    ci = pl.program_id(1)

    @pl.when(ci == 0)
    def _():
        c_s[...] = jnp.zeros_like(c_s)
        n_s[...] = jnp.zeros_like(n_s)
        m_s[...] = jnp.zeros_like(m_s)

    gb = g_ref[...] + bias_ref[...]
    lf = _log_sigmoid(gb)
    hi, mid, lo = _split3(lf)
    tri = tri_ref[...]
    b_all = _dot(tri, hi) + _dot(tri, mid) + _dot(tri, lo)
    b_t = b_all.T
    gb_t = gb.T
    q = q_ref[...]
    k = k_ref[...] * (ML_D ** -0.5)
    v = v_ref[...]
    row = lax.broadcasted_iota(jnp.int32, (L, L), 0)
    col = lax.broadcasted_iota(jnp.int32, (L, L), 1)
    causal = col <= row

    heads = range(ML_HEADS)
    sl = [slice(h * ML_D, (h + 1) * ML_D) for h in heads]
    qb_all, kb_all, vb_all = q.astype(BF16), k.astype(BF16), v.astype(BF16)
    m0 = [m_s[h] for h in heads]
    c0 = [c_s[h] for h in heads]
    n0 = [n_s[h] for h in heads]
    qk = [_dot_nt(qb_all[:, sl[h]], kb_all[:, sl[h]]) for h in heads]
    qc = [_dot_nt(qb_all[:, sl[h]], c0[h].astype(BF16)) for h in heads]
    b_col = [b_all[:, GATE_F + h:GATE_F + h + 1] for h in heads]
    li_col = [gb[:, GATE_I + h:GATE_I + h + 1] for h in heads]
    a_col = [b_col[h] + m0[h] for h in heads]
    dmat = [jnp.where(causal, b_col[h] - b_t[GATE_F + h:GATE_F + h + 1, :] + gb_t[GATE_I + h:GATE_I + h + 1, :],
                      -jnp.inf) for h in heads]
    m_col = [jnp.maximum(a_col[h], jnp.max(dmat[h], axis=1, keepdims=True)) for h in heads]
    ws = [jnp.exp(a_col[h] - m_col[h]) for h in heads]
    s = [qk[h] * jnp.exp(dmat[h] - m_col[h]) for h in heads]
    qn = [jnp.sum(q[:, sl[h]] * n0[h], axis=1, keepdims=True) for h in heads]
    num = [_dot(s[h].astype(BF16), vb_all[:, sl[h]]) + qc[h] * ws[h] for h in heads]
    den = [jnp.sum(s[h], axis=1, keepdims=True) + ws[h] * qn[h] for h in heads]
    for h in heads:
        h_ref[:, sl[h]] = num[h] / jnp.maximum(jnp.abs(den[h]), jnp.exp(-m_col[h]))
    m_end = [m_col[h][L - 1:L, :] for h in heads]
    b_end = [b_col[h][L - 1:L, :] for h in heads]
    w_end = [jnp.exp(b_end[h] - b_col[h] + li_col[h] - m_end[h]) for h in heads]
    decay = [jnp.exp(b_end[h] + m0[h] - m_end[h]) for h in heads]
    vw_t = jnp.concatenate([v[:, sl[h]] * w_end[h] for h in heads], axis=1).T.astype(BF16)
    for h in heads:
        n_s[h] = decay[h] * n0[h] + jnp.sum(w_end[h] * k[:, sl[h]], axis=0, keepdims=True)
        m_s[h] = m_end[h]
        c_s[h] = decay[h] * c0[h] + _dot(vw_t[sl[h], :], kb_all[:, sl[h]])

    @pl.when(ci == pl.num_programs(1) - 1)
    def _():
        c_out[...] = c_s[...]
        n_out[...] = n_s[...]
        m_out[...] = m_s[...]


def mlstm_prompt(h, bias, tri, batch, seq):
    L = ML_L
    nc = seq // L
    qb, kb, vb = (3 * CONV_W + Q_RANK + KV_RANK) // ML_W + np.arange(3)
    gate_blk = IN_MAIN // LANE

    def col(cb):
        return pl.BlockSpec((L, ML_W), lambda b, c: (b * nc + c, cb))

    return pl.pallas_call(
        _mlstm_kernel,
        grid=(batch, nc),
        in_specs=[col(int(qb)), col(int(kb)), col(int(vb)),
                  pl.BlockSpec((L, LANE), lambda b, c: (b * nc + c, gate_blk)),
                  pl.BlockSpec((1, LANE), lambda b, c: (0, 0)),
                  pl.BlockSpec((L, L), lambda b, c: (0, 0))],
        out_specs=[pl.BlockSpec((L, ML_W), lambda b, c: (b * nc + c, 0)),
                   pl.BlockSpec((None, ML_HEADS, ML_D, ML_D), lambda b, c: (b, 0, 0, 0)),
                   pl.BlockSpec((None, ML_HEADS, 1, ML_D), lambda b, c: (b, 0, 0, 0)),
                   pl.BlockSpec((None, ML_HEADS, 1, 1), lambda b, c: (b, 0, 0, 0))],
        out_shape=[jax.ShapeDtypeStruct((batch * seq, ML_W), F32),
                   jax.ShapeDtypeStruct((batch, ML_HEADS, ML_D, ML_D), F32),
                   jax.ShapeDtypeStruct((batch, ML_HEADS, 1, ML_D), F32),
                   jax.ShapeDtypeStruct((batch, ML_HEADS, 1, 1), F32)],
        scratch_shapes=[pltpu.VMEM((ML_HEADS, ML_D, ML_D), F32),
                        pltpu.VMEM((ML_HEADS, 1, ML_D), F32),
                        pltpu.VMEM((ML_HEADS, 1, 1), F32)],
        compiler_params=_cparams(("parallel", "arbitrary")),
        name="mlstm_prompt",
    )(h, h, h, h, bias, tri)


def _mlstm_step_kernel(q_ref, k_ref, vc_ref, li_ref, lfp_ref, c0_ref, n0_ref, m0_ref,
                       h_ref, c1_ref, n1_ref, m1_ref):
    q = q_ref[...]
    k = k_ref[...] * (ML_D ** -0.5)
    vc = vc_ref[...]
    li = li_ref[...]
    lf = _log_sigmoid(lfp_ref[...])
    c0, n0, m0 = c0_ref[...], n0_ref[...], m0_ref[...]
    a = lf + m0
    m = jnp.maximum(a, li)
    w = jnp.exp(li - m)
    ws = jnp.exp(a - m)
    s = jnp.sum(q * k, axis=2, keepdims=True) * w
    cq = jnp.sum(c0 * q, axis=2, keepdims=True)
    num = s * vc + cq * ws
    den = s + ws * jnp.sum(n0 * q, axis=2, keepdims=True)
    h_ref[...] = num / jnp.maximum(jnp.abs(den), jnp.exp(-m))
    c1_ref[...] = ws * c0 + w * (vc * k)
    n1_ref[...] = ws * n0 + w * k
    m1_ref[...] = m


def mlstm_step(q, k, vc, li, lfp, c0, n0, m0):
    g_all = q.shape[0]
    g = 16

    def spec(shape):
        return pl.BlockSpec((g,) + shape, lambda i: (i, 0, 0))

    return pl.pallas_call(
        _mlstm_step_kernel,
        grid=(g_all // g,),
        in_specs=[spec((1, ML_D)), spec((1, ML_D)), spec((ML_D, 1)), spec((1, 1)), spec((1, 1)),
                  spec((ML_D, ML_D)), spec((1, ML_D)), spec((1, 1))],
        out_specs=[spec((ML_D, 1)), spec((ML_D, ML_D)), spec((1, ML_D)), spec((1, 1))],
        out_shape=[jax.ShapeDtypeStruct((g_all, ML_D, 1), F32),
                   jax.ShapeDtypeStruct((g_all, ML_D, ML_D), F32),
                   jax.ShapeDtypeStruct((g_all, 1, ML_D), F32),
                   jax.ShapeDtypeStruct((g_all, 1, 1), F32)],
        compiler_params=_cparams(("parallel",)),
        name="mlstm_step",
    )(q, k, vc, li, lfp, c0, n0, m0)


def _mm_res_kernel(a_ref, b_ref, r_ref, o_ref):
    o_ref[...] = r_ref[...] + _dot(a_ref[...], b_ref[...])


def matmul_residual(a, b, res, tm):
    m, kk = a.shape
    n = b.shape[1]
    return pl.pallas_call(
        _mm_res_kernel,
        grid=(m // tm,),
        in_specs=[pl.BlockSpec((tm, kk), lambda i: (i, 0)),
                  pl.BlockSpec((kk, n), lambda i: (0, 0)),
                  pl.BlockSpec((tm, n), lambda i: (i, 0))],
        out_specs=pl.BlockSpec((tm, n), lambda i: (i, 0)),
        out_shape=jax.ShapeDtypeStruct((m, n), F32),
        compiler_params=_cparams(("parallel",)),
        name="merge_matmul",
    )(a, b, res)


def _merge_kernel(cb_ref, cc_ref, ch_ref, mo_ref, hcc_ref, hch_ref, o_ref, hml_ref, cw_ref, gml_ref, w_ref, x_ref,
                  out_ref, *, tiles_per_seq):
    i = pl.program_id(0)
    u = cc_ref[...] * ch_ref[...]
    hu = jnp.where(i % tiles_per_seq == 0, 0.0, hcc_ref[...] * hch_ref[...])
    row = lax.broadcasted_iota(jnp.int32, u.shape, 0)
    prev1 = jnp.where(row == 0, hu[7:8], pltpu.roll(u, 1, 0))
    prev2 = jnp.where(row == 0, hu[6:7], jnp.where(row == 1, hu[7:8], pltpu.roll(u, 2, 0)))
    cw = cw_ref[...]
    conv = cw[0:1] * prev2 + cw[1:2] * prev1 + cw[2:3] * u
    a = (cb_ref[...] * conv).astype(BF16)

    hml = hml_ref[...]
    lane = lax.broadcasted_iota(jnp.int32, hml.shape, 1)
    sq = hml * hml
    scale = jnp.zeros_like(hml)
    for hh in range(ML_HEADS):
        seg = (lane >= hh * ML_D) & (lane < (hh + 1) * ML_D)
        ms = jnp.sum(jnp.where(seg, sq, 0.0), axis=1, keepdims=True) * (1.0 / ML_D)
        scale = jnp.where(seg, lax.rsqrt(ms + RMS_EPS), scale)
    ml = (hml * scale * gml_ref[...] * jax.nn.sigmoid(mo_ref[...])).astype(BF16)

    out_ref[...] = (x_ref[...] + _dot(a, w_ref[0:CONV_W, :])
                    + _dot(o_ref[...].astype(BF16), w_ref[CONV_W:CONV_W + MLA_W, :])
                    + _dot(ml, w_ref[CONV_W + MLA_W:, :]))


def merge_prompt(h, o_mla, h_ml, cw8, g_ml_row, w_out, x, tm, seq):
    m = x.shape[0]
    nb = seq // tm
    mo_blk = (3 * CONV_W + Q_RANK + KV_RANK + 3 * ML_W) // ML_W
    hb = tm // 8

    def col(cb):
        return pl.BlockSpec((tm, CONV_W), lambda i: (i, cb))

    def halo(cb):
        return pl.BlockSpec((8, CONV_W), lambda i: (jnp.maximum(i * hb - 1, 0), cb))

    return pl.pallas_call(
        functools.partial(_merge_kernel, tiles_per_seq=nb),
        grid=(m // tm,),
        in_specs=[col(0), col(1), col(2), col(mo_blk), halo(1), halo(2),
                  pl.BlockSpec((tm, MLA_W), lambda i: (i, 0)),
                  pl.BlockSpec((tm, ML_W), lambda i: (i, 0)),
                  pl.BlockSpec((8, CONV_W), lambda i: (0, 0)),
                  pl.BlockSpec((1, ML_W), lambda i: (0, 0)),
                  pl.BlockSpec((D_MODEL, D_MODEL), lambda i: (0, 0)),
                  pl.BlockSpec((tm, D_MODEL), lambda i: (i, 0))],
        out_specs=pl.BlockSpec((tm, D_MODEL), lambda i: (i, 0)),
        out_shape=jax.ShapeDtypeStruct((m, D_MODEL), F32),
        compiler_params=_cparams(("parallel",)),
        name="merge_prompt",
    )(h, h, h, h, h, h, o_mla, h_ml, cw8, g_ml_row, w_out, x)


def _bmm_kernel(a_ref, b_ref, o_ref):
    o_ref[...] = _dot(a_ref[...], b_ref[...])


def bmm(a, b):
    nb, m, kk = a.shape
    n = b.shape[2]
    return pl.pallas_call(
        _bmm_kernel,
        grid=(nb,),
        in_specs=[pl.BlockSpec((None, m, kk), lambda i: (i, 0, 0)),
                  pl.BlockSpec((None, kk, n), lambda i: (i, 0, 0))],
        out_specs=pl.BlockSpec((None, m, n), lambda i: (i, 0, 0)),
        out_shape=jax.ShapeDtypeStruct((nb, m, n), F32),
        compiler_params=_cparams(("parallel",)),
        name="bmm",
    )(a, b)


def _decode_kernel(pt_ref, ckv_hbm, kr_hbm, qlat_ref, qrope_ref, cnew_ref, rnew_ref, wukt_ref, o_ref,
                   cbuf, rbuf, sem, m_s, l_s, acc_s, *, layer):
    npg = DEC_PAGES
    nch = npg // DEC_CHUNK
    b, j = pl.program_id(0), pl.program_id(1)
    nj = pl.num_programs(1)
    step = b * nj + j
    last_step = pl.num_programs(0) * nj - 1
    slot = step % 2

    def page_copies(at_step, sl):
        bb, jj = at_step // nj, at_step % nj
        cps = []
        for k in range(npg):
            page = pt_ref[bb, jj * npg + k]
            cps.append(pltpu.make_async_copy(ckv_hbm.at[layer, page],
                                             cbuf.at[sl, pl.ds(k * PAGE_SIZE, PAGE_SIZE)], sem.at[0, sl]))
            cps.append(pltpu.make_async_copy(kr_hbm.at[layer, page], rbuf.at[sl, k], sem.at[1, sl]))
        return cps

    @pl.when(step == 0)
    def _():
        for cp in page_copies(step, slot):
            cp.start()

    for cp in page_copies(step, slot):
        cp.wait()
    prefetch = page_copies(jnp.minimum(step + 1, last_step), 1 - slot)
    per_chunk = 2 * len(prefetch) // nch

    @pl.when(j == 0)
    def _():
        m_s[...] = jnp.full_like(m_s, -jnp.inf)
        l_s[...] = jnp.zeros_like(l_s)
        acc_s[...] = jnp.zeros_like(acc_s)

    lhs = jnp.concatenate([wukt_ref[...], qlat_ref[...]], axis=0)
    qrope = qrope_ref[...]
    nk = MLA_HEADS * NOPE_D

    def products(c, rt):
        cb = c.astype(BF16)
        big = _dot_nt(lhs, cb)
        return cb, big, _dot(qrope, rt.astype(BF16)), jnp.sum(rt * rt, axis=0, keepdims=True)

    def weights(prod, valid, m_old):
        cb, big, s_rope, ssq_r = prod
        n = cb.shape[0]
        ssq = jnp.sum(jnp.square(big[:nk].reshape(NOPE_D, MLA_HEADS, n)), axis=0) + ssq_r
        s = (big[nk:nk + MLA_HEADS] + s_rope[:MLA_HEADS]) * lax.rsqrt(ssq * (1.0 / QK_D) + RMS_EPS)
        if valid is not None:
            s = jnp.where(valid, s, -jnp.inf)
        m_new = jnp.maximum(m_old, jnp.max(s, axis=1, keepdims=True))
        return m_new, jnp.exp2(m_old - m_new), jnp.exp2(s - m_new)

    def accumulate(w, cb, l_old, acc):
        _, alpha, p = w
        return alpha * l_old + jnp.sum(p, axis=1, keepdims=True), alpha * acc + _dot(p.astype(BF16), cb)

    rows = DEC_CHUNK * PAGE_SIZE

    def chunk_products(i):
        c = cbuf[slot, pl.ds(i * rows, rows), :]
        rt = jnp.concatenate([rbuf[slot, i * DEC_CHUNK + k] for k in range(DEC_CHUNK)], axis=1)
        return products(c, rt)

    m_run, l_run, acc = m_s[...], l_s[...], acc_s[...]
    prods = [chunk_products(i) for i in range(min(2, nch))]
    for i in range(nch):
        for cp in prefetch[i * per_chunk:(i + 1) * per_chunk]:
            cp.start()
        w = weights(prods[i], None, m_run)
        if i + 2 < nch:
            prods.append(chunk_products(i + 2))
        l_run, acc = accumulate(w, prods[i][0], l_run, acc)
        m_run = w[0]
    m_s[...], l_s[...], acc_s[...] = m_run, l_run, acc

    @pl.when(j == nj - 1)
    def _():
        lane = lax.broadcasted_iota(jnp.int32, (MLA_HEADS, PAGE_SIZE), 1)
        prod = products(cnew_ref[...], rnew_ref[...])
        l_fin, acc_fin = accumulate(weights(prod, lane == 0, m_run), prod[0], l_run, acc)
        o_ref[...] = acc_fin / l_fin

    @pl.when(step == last_step)
    def _():
        for cp in prefetch:
            cp.wait()


def decode_attention(layer, page_table, cache_ckv, cache_krope_t, qlat, qrope, cnew, rnew_t, wukt):
    nseq, n_pages = page_table.shape
    npg = DEC_PAGES
    nj = n_pages // npg

    in_specs = [pl.BlockSpec(memory_space=pl.ANY),
                pl.BlockSpec(memory_space=pl.ANY),
                pl.BlockSpec((None, Q_ROWS, KV_RANK), lambda b, j, pt: (b, 0, 0)),
                pl.BlockSpec((None, Q_ROWS, ROPE_D), lambda b, j, pt: (b, 0, 0)),
                pl.BlockSpec((None, PAGE_SIZE, KV_RANK), lambda b, j, pt: (b, 0, 0)),
                pl.BlockSpec((None, ROPE_D, PAGE_SIZE), lambda b, j, pt: (b, 0, 0)),
                pl.BlockSpec((MLA_HEADS * NOPE_D, KV_RANK), lambda b, j, pt: (0, 0))]
    return pl.pallas_call(
        functools.partial(_decode_kernel, layer=layer),
        grid_spec=pltpu.PrefetchScalarGridSpec(
            num_scalar_prefetch=1,
            grid=(nseq, nj),
            in_specs=in_specs,
            out_specs=pl.BlockSpec((None, MLA_HEADS, KV_RANK), lambda b, j, pt: (b, 0, 0)),
            scratch_shapes=[pltpu.VMEM((2, npg * PAGE_SIZE, KV_RANK), F32),
                            pltpu.VMEM((2, npg, ROPE_D, PAGE_SIZE), F32),
                            pltpu.SemaphoreType.DMA((2, 2)),
                            pltpu.VMEM((MLA_HEADS, 1), F32),
                            pltpu.VMEM((MLA_HEADS, 1), F32),
                            pltpu.VMEM((MLA_HEADS, KV_RANK), F32)]),
        out_shape=jax.ShapeDtypeStruct((nseq, MLA_HEADS, KV_RANK), F32),
        compiler_params=_cparams(("arbitrary", "arbitrary")),
        name="decode_attention",
    )(page_table, cache_ckv, cache_krope_t, qlat, qrope, cnew, rnew_t, wukt)


def _ffn_kernel(x_ref, g_ref, w1_ref, w3_ref, w2_ref, o_ref, xn_ref):
    f = pl.program_id(1)

    @pl.when(f == 0)
    def _():
        x = x_ref[...]
        xn_ref[...] = _rms(x, g_ref[...]).astype(BF16)
        o_ref[...] = x

    xn = xn_ref[...]
    a = _dot(xn, w1_ref[...])
    b = _dot(xn, w3_ref[...])
    act = (a * jax.nn.sigmoid(a) * b).astype(BF16)
    o_ref[...] += _dot(act, w2_ref[...])


def ffn_dense(x, g, w1, w3, w2, tm, tf):
    m = x.shape[0]
    nf = w1.shape[1] // tf
    return pl.pallas_call(
        _ffn_kernel,
        grid=(m // tm, nf),
        in_specs=[pl.BlockSpec((tm, D_MODEL), lambda i, f: (i, 0)),
                  pl.BlockSpec((1, D_MODEL), lambda i, f: (0, 0)),
                  pl.BlockSpec((D_MODEL, tf), lambda i, f: (0, f)),
                  pl.BlockSpec((D_MODEL, tf), lambda i, f: (0, f)),
                  pl.BlockSpec((tf, D_MODEL), lambda i, f: (f, 0))],
        out_specs=pl.BlockSpec((tm, D_MODEL), lambda i, f: (i, 0)),
        out_shape=jax.ShapeDtypeStruct((m, D_MODEL), F32),
        scratch_shapes=[pltpu.VMEM((tm, D_MODEL), BF16)],
        compiler_params=_cparams(("parallel", "arbitrary")),
        name="ffn_dense",
    )(x, g, w1, w3, w2)


def _router_kernel(x_ref, g_ref, wh_ref, wl_ref, info_ref):
    xn = _rms(x_ref[...], g_ref[...])
    hi, lo = _split2(xn)
    lg =_dot(hi, wh_ref[...]) + _dot(hi, wl_ref[...]) + _dot(lo, wh_ref[...])
    lane = lax.broadcasted_iota(jnp.int32, lg.shape, 1).astype(F32)
    lg = jnp.where(lane < N_EXPERTS, lg, -jnp.inf)
    v1 = jnp.max(lg, axis=1, keepdims=True)
    i1 = jnp.min(jnp.where(lg == v1, lane, float(LANE)), axis=1, keepdims=True)
    rest = jnp.where(lane == i1, -jnp.inf, lg)
    v2 = jnp.max(rest, axis=1, keepdims=True)
    i2 = jnp.min(jnp.where(rest == v2, lane, float(LANE)), axis=1, keepdims=True)
    e = jnp.exp(v2 - v1)
    g1 = 1.0 / (1.0 + e)
    info_ref[...] = jnp.where(lane == 0, i1, jnp.where(lane == 1, i2, jnp.where(lane == 2, g1, e * g1)))


def router(x, g, wr_hi, wr_lo, tm):
    m = x.shape[0]
    return pl.pallas_call(
        _router_kernel,
        grid=(m // tm,),
        in_specs=[pl.BlockSpec((tm, D_MODEL), lambda i: (i, 0)),
                  pl.BlockSpec((1, D_MODEL), lambda i: (0, 0)),
                  pl.BlockSpec((D_MODEL, LANE), lambda i: (0, 0)),
                  pl.BlockSpec((D_MODEL, LANE), lambda i: (0, 0))],
        out_specs=pl.BlockSpec((tm, LANE), lambda i: (i, 0)),
        out_shape=jax.ShapeDtypeStruct((m, LANE), F32),
        compiler_params=_cparams(("parallel",)),
        name="router",
    )(x, g, wr_hi, wr_lo)


def _moe_block_kernel(be_ref, nu_ref, x_ref, g_ref, gate_ref, w1_ref, w3_ref, w2_ref, o_ref):
    del be_ref
    i = pl.program_id(0)

    @pl.when(i < nu_ref[0])
    def _():
        x = _rms(x_ref[...], g_ref[...]).astype(BF16)
        a = _dot(x, w1_ref[...])
        b = _dot(x, w3_ref[...])
        act = (a * jax.nn.sigmoid(a) * b).astype(BF16)
        o_ref[...] = _dot(act, w2_ref[...]) * gate_ref[...]

    @pl.when(i >= nu_ref[0])
    def _():
        o_ref[...] = jnp.zeros_like(o_ref)


def moe_blocks(blk_exp, n_used, xg, g, slot_gate, w1, w3, w2):
    n_slot = xg.shape[0]
    tm = MOE_TM
    return pl.pallas_call(
        _moe_block_kernel,
        grid_spec=pltpu.PrefetchScalarGridSpec(
            num_scalar_prefetch=2,
            grid=(n_slot // tm,),
            in_specs=[pl.BlockSpec((tm, D_MODEL), lambda i, be, nu: (i, 0)),
                      pl.BlockSpec((1, D_MODEL), lambda i, be, nu: (0, 0)),
                      pl.BlockSpec((tm, 1), lambda i, be, nu: (i, 0)),
                      pl.BlockSpec((None, D_MODEL, D_FF_EXPERT), lambda i, be, nu: (be[i], 0, 0)),
                      pl.BlockSpec((None, D_MODEL, D_FF_EXPERT), lambda i, be, nu: (be[i], 0, 0)),
                      pl.BlockSpec((None, D_FF_EXPERT, D_MODEL), lambda i, be, nu: (be[i], 0, 0))],
            out_specs=pl.BlockSpec((tm, D_MODEL), lambda i, be, nu: (i, 0))),
        out_shape=jax.ShapeDtypeStruct((n_slot, D_MODEL), F32),
        compiler_params=_cparams(("arbitrary",)),
        name="moe_blocks",
    )(blk_exp, n_used, xg, g, slot_gate, w1, w3, w2)


def _add3_kernel(a_ref, b_ref, c_ref, o_ref):
    o_ref[...] = a_ref[...] + b_ref[...] + c_ref[...]


def add3(a, b, c, tm):
    m, n = a.shape
    spec = pl.BlockSpec((tm, n), lambda i: (i, 0))
    return pl.pallas_call(
        _add3_kernel,
        grid=(m // tm,),
        in_specs=[spec, spec, spec],
        out_specs=spec,
        out_shape=jax.ShapeDtypeStruct((m, n), F32),
        compiler_params=_cparams(("parallel",)),
        name="moe_combine",
    )(a, b, c)


def _moe_dense_kernel(x_ref, g_ref, gate_ref, w1_ref, w3_ref, w2_ref, o_ref):
    e = pl.program_id(0)

    @pl.when(e == 0)
    def _():
        o_ref[...] = x_ref[...]

    xn = _rms(x_ref[...], g_ref[...]).astype(BF16)
    a = _dot(xn, w1_ref[...])
    b = _dot(xn, w3_ref[...])
    act = (a * jax.nn.sigmoid(a) * b).astype(BF16)
    o_ref[...] += _dot(act, w2_ref[...]) * gate_ref[...]


def moe_dense(x, g, gates, w1, w3, w2):
    m = x.shape[0]
    return pl.pallas_call(
        _moe_dense_kernel,
        grid=(N_EXPERTS,),
        in_specs=[pl.BlockSpec((m, D_MODEL), lambda e: (0, 0)),
                  pl.BlockSpec((1, D_MODEL), lambda e: (0, 0)),
                  pl.BlockSpec((None, m, 1), lambda e: (e, 0, 0)),
                  pl.BlockSpec((None, D_MODEL, D_FF_EXPERT), lambda e: (e, 0, 0)),
                  pl.BlockSpec((None, D_MODEL, D_FF_EXPERT), lambda e: (e, 0, 0)),
                  pl.BlockSpec((None, D_FF_EXPERT, D_MODEL), lambda e: (e, 0, 0))],
        out_specs=pl.BlockSpec((m, D_MODEL), lambda e: (0, 0)),
        out_shape=jax.ShapeDtypeStruct((m, D_MODEL), F32),
        compiler_params=_cparams(("arbitrary",)),
        name="moe_dense",
    )(x, g, gates, w1, w3, w2)


def _rope_tables(pos, rope_lane0):
    inv_freq = ROPE_THETA ** (-jnp.arange(HALF_ROPE, dtype=F32) / HALF_ROPE)
    ang = pos.astype(F32)[:, None] * inv_freq[None, :]
    cos, sin = jnp.cos(ang), jnp.sin(ang)
    n = pos.shape[0]
    z = lambda w: jnp.zeros((n, w), F32)
    tail = LANE - rope_lane0 - ROPE_D
    c = jnp.concatenate([jnp.ones((n, rope_lane0), F32), cos, cos, z(tail)], axis=1)
    s1 = jnp.concatenate([z(rope_lane0), -sin, z(HALF_ROPE), z(tail)], axis=1)
    s2 = jnp.concatenate([z(rope_lane0), z(HALF_ROPE), sin, z(tail)], axis=1)
    return c, s1, s2


def _head_gain(g):
    return jnp.concatenate([g[:NOPE_D], g[NOPE_D:], g[NOPE_D:], jnp.zeros((HP - QK_D,), F32)])[None, :]


def _prep_layer(l, w_in, g_cq, w_uq, g_ckv, w_uk, w_uv, g_q, g_k, b_i, b_f, w_out):
    d = {}
    wi = w_in[l]
    kr0 = 3 * CONV_W + Q_RANK + KV_RANK
    d["w_in"] = jnp.concatenate(
        [wi[:, :kr0], wi[:, kr0 + ROPE_D:kr0 + ROPE_D + 4 * ML_W], wi[:, kr0:kr0 + ROPE_D],
         wi[:, kr0 + ROPE_D + 4 * ML_W:], jnp.zeros((D_MODEL, LANE - ROPE_D - 2 * ML_HEADS), F32)],
        axis=1).astype(BF16)
    wq = w_uq[l].reshape(Q_RANK, MLA_HEADS, QK_D)
    d["w_uq"] = jnp.pad(wq, ((0, 0), (0, 0), (0, HP - QK_D))).reshape(Q_RANK, MLA_HEADS * HP).astype(BF16)
    wk = jnp.pad(w_uk[l], ((0, 0), (0, 0), (0, HP - NOPE_D))).reshape(KV_RANK, MLA_HEADS * HP)
    place = jnp.zeros((LANE, MLA_HEADS, HP), F32)
    place = place.at[jnp.arange(ROPE_D)[:, None], jnp.arange(MLA_HEADS)[None, :],
                     NOPE_D + jnp.arange(ROPE_D)[:, None]].set(1.0)
    d["w_k"] = jnp.concatenate([wk, place.reshape(LANE, MLA_HEADS * HP)], axis=0).astype(BF16)
    d["w_vt"] = w_uv[l].reshape(KV_RANK, MLA_W).T.astype(BF16)
    d["w_ukt_dec"] = w_uk[l].transpose(2, 1, 0).reshape(NOPE_D * MLA_HEADS, KV_RANK).astype(BF16)
    d["w_uk_h"] = w_uk[l].transpose(1, 2, 0).astype(BF16)
    d["w_uv_h"] = w_uv[l].transpose(1, 0, 2).astype(BF16)
    d["g_cq"] = g_cq[l][None, :]
    d["g_ckv"] = g_ckv[l][None, :]
    d["g_q"] = _head_gain(g_q[l])
    d["g_k"] = _head_gain(g_k[l])
    bias = jnp.zeros((LANE,), F32).at[GATE_I:GATE_I + ML_HEADS].set(b_i[l])
    d["gate_bias"] = bias.at[GATE_F:GATE_F + ML_HEADS].set(b_f[l])[None, :]
    d["w_out"] = w_out[l].astype(BF16)
    return d


def _split_h(h):
    o = 0
    out = []
    for w in (CONV_W, CONV_W, CONV_W, Q_RANK, KV_RANK, ML_W, ML_W, ML_W, ML_W):
        out.append(h[:, o:o + w])
        o += w
    return out


def _ml_merge(h_ml, mo, g_ml_l):
    hm = h_ml.reshape(-1, ML_HEADS, ML_D)
    ml = hm * lax.rsqrt(jnp.mean(hm * hm, axis=-1, keepdims=True) + RMS_EPS) * g_ml_l[None]
    return ml.reshape(-1, ML_W) * jax.nn.sigmoid(mo)


def _route(info):
    return info[:, TOP_K:2 * TOP_K], info[:, :TOP_K].astype(jnp.int32)


def kernel(x_prompt, x_sample, cache_ckv, cache_krope, page_table, state_conv, state_C, state_n, state_m,
           g_norm_mix, g_norm_ffn, w_in, conv_w, g_cq, w_uq, g_ckv, w_uk, w_uv, g_q, g_k, b_i, b_f, g_ml,
           w_out, w_ff1, w_ff3, w_ff2, w_router, w_e1, w_e3, w_e2):
    batch, seq, _ = x_prompt.shape
    nseq = x_sample.shape[0]
    depth = w_in.shape[0]
    past_len = page_table.shape[1] * PAGE_SIZE
    mp = batch * seq
    tm_p = 512
    assert seq % max(tm_p, ATTN_T, ML_L) == 0 and x_sample.shape[1] == 1
    assert page_table.shape[1] % DEC_PAGES == 0 and DEC_PAGES % (2 * DEC_CHUNK) == 0
    assert nseq % 8 == 0 and (nseq * ML_HEADS) % 16 == 0

    xp = x_prompt.reshape(mp, D_MODEL)
    xs = x_sample.reshape(nseq, D_MODEL)

    pos_p = jnp.arange(seq)
    pos_s = jnp.full((nseq,), past_len)
    tabs_q_p = _rope_tables(pos_p, NOPE_D)
    tabs_k_p = _rope_tables(pos_p, 0)
    tabs_q_s = _rope_tables(pos_s, NOPE_D)
    tabs_k_s = _rope_tables(pos_s, 0)
    tri = jnp.tril(jnp.ones((ML_L, ML_L), F32)).astype(BF16)
    scale_row = jnp.full((1, HP), ATTN_SCALE * LOG2E, F32)
    cache_krope_t = cache_krope.transpose(0, 1, 3, 2)

    outs = {k: [] for k in ("ckv_p", "ckv_s", "kr_p", "kr_s", "cv_p", "cv_s",
                            "C_p", "C_s", "n_p", "n_s", "m_p", "m_s")}

    for l in range(depth):
        p = _prep_layer(l, w_in, g_cq, w_uq, g_ckv, w_uk, w_uv, g_q, g_k, b_i, b_f, w_out)
        gmix = g_norm_mix[l][None, :]
        gffn = g_norm_ffn[l][None, :]
        cw = conv_w[l]

        h = inproj(xp, gmix, p["w_in"], tm_p)
        q = q_path(h, p["g_cq"], p["w_uq"], p["g_q"] * scale_row, tabs_q_p, tm_p, seq // tm_p)
        ckvn, krr = ckv_path(h, p["g_ckv"], tabs_k_p, tm_p, seq // tm_p)
        k, vt = kv_path(ckvn, krr, p["w_k"], p["w_vt"], p["g_k"], batch, seq)
        o_mla = prompt_attention(q.reshape(batch, seq, -1), k.reshape(batch, seq, -1), vt, batch, seq)
        h_ml, c1, n1, m1 = mlstm_prompt(h, p["gate_bias"], tri, batch, seq)

        cw8 = jnp.pad(cw, ((0, 8 - CONV_K), (0, 0)))
        xp = merge_prompt(h, o_mla.reshape(mp, MLA_W), h_ml, cw8, g_ml[l].reshape(1, ML_W), p["w_out"], xp,
                          tm_p, seq)
        tail = h.reshape(batch, seq, IN_PAD)[:, seq - (CONV_K - 1):, CONV_W:3 * CONV_W]

        outs["ckv_p"].append(ckvn.reshape(batch, seq, KV_RANK))
        outs["kr_p"].append(krr[:, :ROPE_D].reshape(batch, seq, ROPE_D))
        outs["cv_p"].append(tail[..., :CONV_W] * tail[..., CONV_W:])
        outs["C_p"].append(c1)
        outs["n_p"].append(n1.reshape(batch, ML_HEADS, ML_D))
        outs["m_p"].append(m1.reshape(batch, ML_HEADS))

        hs = inproj(xs, gmix, p["w_in"], nseq)
        cb, cc, ch, _, _, mq, mk, mv, mo = _split_h(hs)
        gates = hs[:, IN_MAIN:]
        qs = q_path(hs, p["g_cq"], p["w_uq"], p["g_q"] * p["g_k"] * scale_row, tabs_q_s, nseq, 1)
        ckvn_s, krr_s = ckv_path(hs, p["g_ckv"], tabs_k_s, nseq, 1)
        qs3 = qs.reshape(nseq, MLA_HEADS, HP)
        qlat = bmm(qs3[:, :, :NOPE_D].transpose(1, 0, 2), p["w_uk_h"])
        qpad = ((0, 0), (0, Q_ROWS - MLA_HEADS), (0, 0))
        qlat = jnp.pad(qlat.transpose(1, 0, 2).astype(BF16), qpad)
        qrope = jnp.pad(qs3[:, :, NOPE_D:QK_D], qpad)
        cnew = jnp.pad(ckvn_s[:, None, :], ((0, 0), (0, PAGE_SIZE - 1), (0, 0)))
        rnew_t = jnp.pad(krr_s[:, :ROPE_D, None], ((0, 0), (0, 0), (0, PAGE_SIZE - 1)))
        pc = decode_attention(l, page_table, cache_ckv, cache_krope_t, qlat, qrope, cnew, rnew_t, p["w_ukt_dec"])
        o_s = bmm(pc.transpose(1, 0, 2).astype(BF16), p["w_uv_h"])
        o_s = o_s.transpose(1, 0, 2).reshape(nseq, MLA_W)

        g_all = nseq * ML_HEADS
        li = (gates[:, GATE_I:GATE_I + ML_HEADS] + b_i[l][None, :]).reshape(g_all, 1, 1)
        lfp = (gates[:, GATE_F:GATE_F + ML_HEADS] + b_f[l][None, :]).reshape(g_all, 1, 1)
        h_col, c1s, n1s, m1s = mlstm_step(
            mq.reshape(g_all, 1, ML_D), mk.reshape(g_all, 1, ML_D), mv.reshape(g_all, ML_D, 1), li, lfp,
            state_C[l].reshape(g_all, ML_D, ML_D), state_n[l].reshape(g_all, 1, ML_D),
            state_m[l].reshape(g_all, 1, 1))

        us = cc * ch
        st = state_conv[l]
        conv_s = cw[0] * st[:, 0] + cw[1] * st[:, 1] + cw[2] * us
        mix = jnp.concatenate([cb * conv_s, o_s, _ml_merge(h_col.reshape(nseq, ML_W), mo, g_ml[l])], axis=1)
        xs = matmul_residual(mix.astype(BF16), p["w_out"], xs, nseq)

        outs["ckv_s"].append(ckvn_s[:, None, :])
        outs["kr_s"].append(krr_s[:, None, :ROPE_D])
        outs["cv_s"].append(jnp.stack([st[:, 1], us], axis=1))
        outs["C_s"].append(c1s.reshape(nseq, ML_HEADS, ML_D, ML_D))
        outs["n_s"].append(n1s.reshape(nseq, ML_HEADS, ML_D))
        outs["m_s"].append(m1s.reshape(nseq, ML_HEADS))

        j = l // 2
        if l % 2 == 0:
            w1, w3, w2 = w_ff1[j].astype(BF16), w_ff3[j].astype(BF16), w_ff2[j].astype(BF16)
            xp = ffn_dense(xp, gffn, w1, w3, w2, tm_p, D_FF // 2)
            xs = ffn_dense(xs, gffn, w1, w3, w2, nseq, D_FF // 2)
        else:
            w1, w3, w2 = w_e1[j].astype(BF16), w_e3[j].astype(BF16), w_e2[j].astype(BF16)
            wr = jnp.pad(w_router[j], ((0, 0), (0, LANE - N_EXPERTS)))
            wr_hi = wr.astype(BF16)
            wr_lo = (wr - wr_hi.astype(F32)).astype(BF16)

            gate, top_i = _route(router(xp, gffn, wr_hi, wr_lo, tm_p))
            e_flat = top_i.reshape(-1)
            n_assign = mp * TOP_K
            onehot = (e_flat[:, None] == jnp.arange(N_EXPERTS)[None, :]).astype(jnp.int32)
            rank = jnp.take_along_axis(jnp.cumsum(onehot, axis=0) - onehot, e_flat[:, None], axis=1)[:, 0]
            counts = jnp.sum(onehot, axis=0)
            padded = (counts + MOE_TM - 1) // MOE_TM * MOE_TM
            pad_end = jnp.cumsum(padded)
            pad_start = pad_end - padded
            dest = pad_start[e_flat] + rank
            n_blk = (n_assign + N_EXPERTS * (MOE_TM - 1) + MOE_TM - 1) // MOE_TM
            n_slot = n_blk * MOE_TM
            slot_asg = jnp.full((n_slot,), -1, jnp.int32).at[dest].set(jnp.arange(n_assign, dtype=jnp.int32))
            slot_tok = jnp.maximum(slot_asg, 0) // TOP_K
            slot_gate = jnp.where(slot_asg >= 0, gate.reshape(-1)[jnp.maximum(slot_asg, 0)], 0.0)
            blk_start = jnp.arange(n_blk, dtype=jnp.int32) * MOE_TM
            blk_exp = jnp.minimum(jnp.sum((pad_end[None, :] <= blk_start[:, None]).astype(jnp.int32), axis=1),
                                  N_EXPERTS - 1)
            n_used = (pad_end[-1:] // MOE_TM).astype(jnp.int32)
            yb = moe_blocks(blk_exp, n_used, xp[slot_tok], gffn, slot_gate[:, None], w1, w3, w2)
            dest2 = dest.reshape(mp, TOP_K)
            xp = add3(xp, yb[dest2[:, 0]], yb[dest2[:, 1]], tm_p)

            gate_s, top_s = _route(router(xs, gffn, wr_hi, wr_lo, nseq))
            dense_gate = jnp.sum(
                (top_s[:, :, None] == jnp.arange(N_EXPERTS)[None, None, :]) * gate_s[:, :, None], axis=1)
            xs = moe_dense(xs, gffn, dense_gate.T[:, :, None], w1, w3, w2)

    return (xp.reshape(batch, seq, D_MODEL), xs.reshape(nseq, 1, D_MODEL),
            jnp.stack(outs["ckv_p"]), jnp.stack(outs["ckv_s"]),
            jnp.stack(outs["kr_p"]), jnp.stack(outs["kr_s"]),
            jnp.stack(outs["cv_p"]), jnp.stack(outs["cv_s"]),
            jnp.stack(outs["C_p"]), jnp.stack(outs["C_s"]),
            jnp.stack(outs["n_p"]), jnp.stack(outs["n_s"]),
            jnp.stack(outs["m_p"]), jnp.stack(outs["m_s"]))
```

```python
import functools
import math

import jax
import jax.numpy as jnp
import numpy as np
from jax import lax
from jax.experimental import pallas as pl
from jax.experimental.pallas import tpu as pltpu

F32 = jnp.float32
BF16 = jnp.bfloat16

D_MODEL = 1024
PAGE_SIZE = 128
CONV_W = 256
CONV_K = 3
MLA_HEADS = 8
NOPE_D = 64
ROPE_D = 32
HALF_ROPE = ROPE_D // 2
QK_D = NOPE_D + ROPE_D
V_D = 64
MLA_W = MLA_HEADS * V_D
Q_RANK = 384
KV_RANK = 128
ROPE_THETA = 10000.0
ATTN_SCALE = QK_D ** -0.5
ML_HEADS = 4
ML_D = 64
ML_W = ML_HEADS * ML_D
D_FF = 2816
N_EXPERTS = 8
TOP_K = 2
D_FF_EXPERT = 1408
RMS_EPS = 1e-6
LOG2E = math.log2(math.e)

LANE = 128
HP = LANE
IN_MAIN = 2304
IN_PAD = IN_MAIN + LANE
GATE_I = ROPE_D
GATE_F = ROPE_D + ML_HEADS
VMEM_LIMIT = 56 * 1024 * 1024

ATTN_T = 512
ATTN_HG = 2
ATTN_SUM_ROWS = 16
ATTN_CK = 256
ML_L = 128
DEC_PAGES = 128
DEC_CHUNK = 16
Q_ROWS = 16
MOE_TM = 512


def _cparams(sem):
    return pltpu.CompilerParams(dimension_semantics=sem, vmem_limit_bytes=VMEM_LIMIT)


def _rms(x, g):
    return x * lax.rsqrt(jnp.mean(x * x, axis=-1, keepdims=True) + RMS_EPS) * g


def _dot(a, b):
    return jnp.dot(a, b, preferred_element_type=F32)


def _dot_nt(a, b):
    return lax.dot_general(a, b, (((1,), (1,)), ((), ())), preferred_element_type=F32)


def _split2(x):
    hi = x.astype(BF16)
    lo = (x - hi.astype(F32)).astype(BF16)
    return hi, lo


def _split3(x):
    hi = x.astype(BF16)
    r = x - hi.astype(F32)
    mid = r.astype(BF16)
    lo = (r - mid.astype(F32)).astype(BF16)
    return hi, mid, lo


def _inproj_kernel(x_ref, g_ref, w_ref, o_ref):
    xn = _rms(x_ref[...], g_ref[...]).astype(BF16)
    o_ref[...] = _dot(xn, w_ref[...])


def inproj(x, g, w, tm):
    m = x.shape[0]
    n = w.shape[1]
    return pl.pallas_call(
        _inproj_kernel,
        grid=(m // tm,),
        in_specs=[pl.BlockSpec((tm, D_MODEL), lambda i: (i, 0)),
                  pl.BlockSpec((1, D_MODEL), lambda i: (0, 0)),
                  pl.BlockSpec((D_MODEL, n), lambda i: (0, 0))],
        out_specs=pl.BlockSpec((tm, n), lambda i: (i, 0)),
        out_shape=jax.ShapeDtypeStruct((m, n), F32),
        compiler_params=_cparams(("parallel",)),
        name="inproj",
    )(x, g, w)


def _q_kernel(cq_ref, gcq_ref, wuq_ref, gq_ref, c_ref, s1_ref, s2_ref, o_ref):
    cqn = _rms(cq_ref[...], gcq_ref[...]).astype(BF16)
    q = _dot(cqn, wuq_ref[...])
    c, s1, s2, gq = c_ref[...], s1_ref[...], s2_ref[...], gq_ref[...]
    for h in range(MLA_HEADS):
        qh = q[:, h * HP:(h + 1) * HP]
        qh = qh * c + pltpu.roll(qh, HP - HALF_ROPE, 1) * s1 + pltpu.roll(qh, HALF_ROPE, 1) * s2
        ssq = jnp.sum(qh * qh, axis=-1, keepdims=True)
        qh = qh * lax.rsqrt(ssq * (1.0 / QK_D) + RMS_EPS) * gq
        o_ref[:, h * HP:(h + 1) * HP] = qh.astype(o_ref.dtype)


def q_path(h, gcq, wuq, gq, tabs, tm, n_pos_blk):
    m = h.shape[0]
    cq_blk = (3 * CONV_W) // Q_RANK
    tab_spec = pl.BlockSpec((tm, HP), lambda i: (i % n_pos_blk, 0))
    return pl.pallas_call(
        _q_kernel,
        grid=(m // tm,),
        in_specs=[pl.BlockSpec((tm, Q_RANK), lambda i: (i, cq_blk)),
                  pl.BlockSpec((1, Q_RANK), lambda i: (0, 0)),
                  pl.BlockSpec((Q_RANK, MLA_HEADS * HP), lambda i: (0, 0)),
                  pl.BlockSpec((1, HP), lambda i: (0, 0)),
                  tab_spec, tab_spec, tab_spec],
        out_specs=pl.BlockSpec((tm, MLA_HEADS * HP), lambda i: (i, 0)),
        out_shape=jax.ShapeDtypeStruct((m, MLA_HEADS * HP), BF16),
        compiler_params=_cparams(("parallel",)),
        name="q_path",
    )(h, gcq, wuq, gq, *tabs)


def _ckv_kernel(ckv_ref, kr_ref, g_ref, c_ref, s1_ref, s2_ref, ckvn_ref, krr_ref):
    ckvn_ref[...] = _rms(ckv_ref[...], g_ref[...])
    kr = kr_ref[...]
    krr_ref[...] = (kr * c_ref[...] + pltpu.roll(kr, LANE - HALF_ROPE, 1) * s1_ref[...]
                    + pltpu.roll(kr, HALF_ROPE, 1) * s2_ref[...])


def ckv_path(h, g_ckv, tabs, tm, n_pos_blk):
    m = h.shape[0]
    ckv_blk = (3 * CONV_W + Q_RANK) // KV_RANK
    kr_blk = IN_MAIN // LANE
    tab_spec = pl.BlockSpec((tm, LANE), lambda i: (i % n_pos_blk, 0))
    return pl.pallas_call(
        _ckv_kernel,
        grid=(m // tm,),
        in_specs=[pl.BlockSpec((tm, KV_RANK), lambda i: (i, ckv_blk)),
                  pl.BlockSpec((tm, LANE), lambda i: (i, kr_blk)),
                  pl.BlockSpec((1, KV_RANK), lambda i: (0, 0)),
                  tab_spec, tab_spec, tab_spec],
        out_specs=[pl.BlockSpec((tm, KV_RANK), lambda i: (i, 0)),
                   pl.BlockSpec((tm, LANE), lambda i: (i, 0))],
        out_shape=[jax.ShapeDtypeStruct((m, KV_RANK), F32),
                   jax.ShapeDtypeStruct((m, LANE), F32)],
        compiler_params=_cparams(("parallel",)),
        name="ckv_path",
    )(h, h, g_ckv, *tabs)


def _kv_kernel(ckvn_ref, krr_ref, wk_ref, wvt_ref, gk_ref, k_ref, vt_ref):
    cb = ckvn_ref[...].astype(BF16)
    ck = jnp.concatenate([cb, krr_ref[...].astype(BF16)], axis=1)
    k = _dot(ck, wk_ref[...])
    gk = gk_ref[...]
    for h in range(MLA_HEADS):
        kh = k[:, h * HP:(h + 1) * HP]
        ssq = jnp.sum(kh * kh, axis=-1, keepdims=True)
        k_ref[:, h * HP:(h + 1) * HP] = (kh * lax.rsqrt(ssq * (1.0 / QK_D) + RMS_EPS) * gk).astype(BF16)
    vt_ref[...] = _dot_nt(wvt_ref[...], cb).astype(BF16)


def kv_path(ckvn, krr, wk, wvt, gk, batch, seq):
    tm = ATTN_T
    m = ckvn.shape[0]
    nb = seq // tm
    return pl.pallas_call(
        _kv_kernel,
        grid=(m // tm,),
        in_specs=[pl.BlockSpec((tm, KV_RANK), lambda i: (i, 0)),
                  pl.BlockSpec((tm, LANE), lambda i: (i, 0)),
                  pl.BlockSpec((2 * LANE, MLA_HEADS * HP), lambda i: (0, 0)),
                  pl.BlockSpec((MLA_W, KV_RANK), lambda i: (0, 0)),
                  pl.BlockSpec((1, HP), lambda i: (0, 0))],
        out_specs=[pl.BlockSpec((tm, MLA_HEADS * HP), lambda i: (i, 0)),
                   pl.BlockSpec((None, None, MLA_W, tm), lambda i: (i // nb, i % nb, 0, 0))],
        out_shape=[jax.ShapeDtypeStruct((m, MLA_HEADS * HP), BF16),
                   jax.ShapeDtypeStruct((batch, nb, MLA_W, tm), BF16)],
        compiler_params=_cparams(("parallel",)),
        name="kv_path",
    )(ckvn, krr, wk, wvt, gk)


def _attn_kernel(q_ref, k_ref, vt_ref, o_ref, s0_ref, s1_ref):
    t = ATTN_T
    hg = ATTN_HG
    qi = pl.program_id(2)
    row = lax.broadcasted_iota(jnp.int32, (t, t), 0)
    col = lax.broadcasted_iota(jnp.int32, (t, t), 1)
    causal = row <= col
    ones_rows = jnp.ones((ATTN_SUM_ROWS, t), BF16)

    def qk(j, dst, masked):
        cms = []
        for hh in range(hg):
            k = k_ref[pl.ds(pl.multiple_of(j * t, t), t), hh * HP:(hh + 1) * HP]
            st = _dot_nt(k, q_ref[:, hh * HP:(hh + 1) * HP])
            st = jnp.where(causal, st, -jnp.inf) if masked else st
            dst[hh] = st
            cms.append(jnp.max(st, axis=0, keepdims=True))
        return tuple(cms)

    def process(j, src, cms, carry):
        out = []
        for hh in range(hg):
            m, acc = carry[2 * hh:2 * hh + 2]
            m_new = jnp.maximum(m, cms[hh])
            p = jnp.exp2(src[hh] - m_new).astype(BF16)
            alpha = jnp.exp2(m - m_new)
            va = jnp.concatenate([vt_ref[j, hh * V_D:(hh + 1) * V_D, :], ones_rows], axis=0)
            out += [m_new, alpha * acc + _dot(va, p)]
        return tuple(out)

    def fused(jq, dst, jp, src, cms, carry):
        ck = ATTN_CK
        out, new_cms = [], []
        for hh in range(hg):
            m, acc = carry[2 * hh:2 * hh + 2]
            m_new = jnp.maximum(m, cms[hh])
            acc = jnp.exp2(m - m_new) * acc
            k = k_ref[pl.ds(pl.multiple_of(jq * t, t), t), hh * HP:(hh + 1) * HP]
            st = _dot_nt(k, q_ref[:, hh * HP:(hh + 1) * HP])
            dst[hh] = st
            cm = jnp.max(st, axis=0, keepdims=True)
            for c in range(t // ck):
                p = jnp.exp2(src[hh, c * ck:(c + 1) * ck, :] - m_new).astype(BF16)
                va = jnp.concatenate([vt_ref[jp, hh * V_D:(hh + 1) * V_D, c * ck:(c + 1) * ck],
                                      ones_rows[:, :ck]], axis=0)
                acc = acc + _dot(va, p)
            new_cms.append(cm)
            out += [m_new, acc]
        return tuple(out), tuple(new_cms)

    ns = 2 * hg
    cm0 = qk(qi, s0_ref, True)
    init = (jnp.full((1, t), -jnp.inf, F32), jnp.zeros((V_D + ATTN_SUM_ROWS, t), F32)) * hg + cm0

    def body(tt, carry):
        state, cm1 = fused(2 * tt, s1_ref, jnp.where(tt == 0, qi, 2 * tt - 1), s0_ref, carry[ns:], carry[:ns])
        state, cm0n = fused(jnp.minimum(2 * tt + 1, jnp.maximum(qi - 1, 0)), s0_ref, 2 * tt, s1_ref, cm1, state)
        return state + cm0n

    carry = lax.fori_loop(0, (qi + 1) // 2, body, init)
    carry = lax.cond(qi % 2 == 0,
                     lambda c: process(jnp.where(qi == 0, 0, qi - 1), s0_ref, c[ns:], c[:ns]),
                     lambda c: c[:ns], carry)
    o_t = jnp.concatenate([carry[2 * hh + 1][:V_D] / carry[2 * hh + 1][V_D:V_D + 1] for hh in range(hg)], axis=0)
    o_ref[...] = o_t.T


def prompt_attention(q, k, vt, batch, seq):
    t = ATTN_T
    hg = ATTN_HG
    nb = seq // t
    return pl.pallas_call(
        _attn_kernel,
        grid=(batch, MLA_HEADS // hg, nb),
        in_specs=[pl.BlockSpec((None, t, hg * HP), lambda b, p, i: (b, i, p)),
                  pl.BlockSpec((None, seq, hg * HP), lambda b, p, i: (b, 0, p)),
                  pl.BlockSpec((None, nb, hg * V_D, t), lambda b, p, i: (b, 0, p, 0))],
        out_specs=pl.BlockSpec((None, t, hg * V_D), lambda b, p, i: (b, i, p)),
        out_shape=jax.ShapeDtypeStruct((batch, seq, MLA_W), F32),
        scratch_shapes=[pltpu.VMEM((hg, t, t), F32), pltpu.VMEM((hg, t, t), F32)],
        compiler_params=_cparams(("parallel", "parallel", "arbitrary")),
        name="prompt_attention",
    )(q, k, vt)


def _log_sigmoid(x):
    return jnp.minimum(x, 0.0) - jnp.log1p(jnp.exp(-jnp.abs(x)))


def _mlstm_kernel(q_ref, k_ref, v_ref, g_ref, bias_ref, tri_ref,
                  h_ref, c_out, n_out, m_out, c_s, n_s, m_s):
    L = ML_L
    ci = pl.program_id(1)

    @pl.when(ci == 0)
    def _():
        c_s[...] = jnp.zeros_like(c_s)
        n_s[...] = jnp.zeros_like(n_s)
        m_s[...] = jnp.zeros_like(m_s)

    gb = g_ref[...] + bias_ref[...]
    lf = _log_sigmoid(gb)
    hi, mid, lo = _split3(lf)
    tri = tri_ref[...]
    b_all = _dot(tri, hi) + _dot(tri, mid) + _dot(tri, lo)
    b_t = b_all.T
    gb_t = gb.T
    q = q_ref[...]
    k = k_ref[...] * (ML_D ** -0.5)
    v = v_ref[...]
    row = lax.broadcasted_iota(jnp.int32, (L, L), 0)
    col = lax.broadcasted_iota(jnp.int32, (L, L), 1)
    causal = col <= row

    heads = range(ML_HEADS)
    sl = [slice(h * ML_D, (h + 1) * ML_D) for h in heads]
    qb_all, kb_all, vb_all = q.astype(BF16), k.astype(BF16), v.astype(BF16)
    m0 = [m_s[h] for h in heads]
    c0 = [c_s[h] for h in heads]
    n0 = [n_s[h] for h in heads]
    qk = [_dot_nt(qb_all[:, sl[h]], kb_all[:, sl[h]]) for h in heads]
    qc = [_dot_nt(qb_all[:, sl[h]], c0[h].astype(BF16)) for h in heads]
    b_col = [b_all[:, GATE_F + h:GATE_F + h + 1] for h in heads]
    li_col = [gb[:, GATE_I + h:GATE_I + h + 1] for h in heads]
    a_col = [b_col[h] + m0[h] for h in heads]
    dmat = [jnp.where(causal, b_col[h] - b_t[GATE_F + h:GATE_F + h + 1, :] + gb_t[GATE_I + h:GATE_I + h + 1, :],
                      -jnp.inf) for h in heads]
    m_col = [jnp.maximum(a_col[h], jnp.max(dmat[h], axis=1, keepdims=True)) for h in heads]
    ws = [jnp.exp(a_col[h] - m_col[h]) for h in heads]
    s = [qk[h] * jnp.exp(dmat[h] - m_col[h]) for h in heads]
    qn = [jnp.sum(q[:, sl[h]] * n0[h], axis=1, keepdims=True) for h in heads]
    num = [_dot(s[h].astype(BF16), vb_all[:, sl[h]]) + qc[h] * ws[h] for h in heads]
    den = [jnp.sum(s[h], axis=1, keepdims=True) + ws[h] * qn[h] for h in heads]
    for h in heads:
        h_ref[:, sl[h]] = num[h] / jnp.maximum(jnp.abs(den[h]), jnp.exp(-m_col[h]))
    m_end = [m_col[h][L - 1:L, :] for h in heads]
    b_end = [b_col[h][L - 1:L, :] for h in heads]
    w_end = [jnp.exp(b_end[h] - b_col[h] + li_col[h] - m_end[h]) for h in heads]
    decay = [jnp.exp(b_end[h] + m0[h] - m_end[h]) for h in heads]
    vw_t = jnp.concatenate([v[:, sl[h]] * w_end[h] for h in heads], axis=1).T.astype(BF16)
    for h in heads:
        n_s[h] = decay[h] * n0[h] + jnp.sum(w_end[h] * k[:, sl[h]], axis=0, keepdims=True)
        m_s[h] = m_end[h]
        c_s[h] = decay[h] * c0[h] + _dot(vw_t[sl[h], :], kb_all[:, sl[h]])

    @pl.when(ci == pl.num_programs(1) - 1)
    def _():
        c_out[...] = c_s[...]
        n_out[...] = n_s[...]
        m_out[...] = m_s[...]


def mlstm_prompt(h, bias, tri, batch, seq):
    L = ML_L
    nc = seq // L
    qb, kb, vb = (3 * CONV_W + Q_RANK + KV_RANK) // ML_W + np.arange(3)
    gate_blk = IN_MAIN // LANE

    def col(cb):
        return pl.BlockSpec((L, ML_W), lambda b, c: (b * nc + c, cb))

    return pl.pallas_call(
        _mlstm_kernel,
        grid=(batch, nc),
        in_specs=[col(int(qb)), col(int(kb)), col(int(vb)),
                  pl.BlockSpec((L, LANE), lambda b, c: (b * nc + c, gate_blk)),
                  pl.BlockSpec((1, LANE), lambda b, c: (0, 0)),
                  pl.BlockSpec((L, L), lambda b, c: (0, 0))],
        out_specs=[pl.BlockSpec((L, ML_W), lambda b, c: (b * nc + c, 0)),
                   pl.BlockSpec((None, ML_HEADS, ML_D, ML_D), lambda b, c: (b, 0, 0, 0)),
                   pl.BlockSpec((None, ML_HEADS, 1, ML_D), lambda b, c: (b, 0, 0, 0)),
                   pl.BlockSpec((None, ML_HEADS, 1, 1), lambda b, c: (b, 0, 0, 0))],
        out_shape=[jax.ShapeDtypeStruct((batch * seq, ML_W), F32),
                   jax.ShapeDtypeStruct((batch, ML_HEADS, ML_D, ML_D), F32),
                   jax.ShapeDtypeStruct((batch, ML_HEADS, 1, ML_D), F32),
                   jax.ShapeDtypeStruct((batch, ML_HEADS, 1, 1), F32)],
        scratch_shapes=[pltpu.VMEM((ML_HEADS, ML_D, ML_D), F32),
                        pltpu.VMEM((ML_HEADS, 1, ML_D), F32),
                        pltpu.VMEM((ML_HEADS, 1, 1), F32)],
        compiler_params=_cparams(("parallel", "arbitrary")),
        name="mlstm_prompt",
    )(h, h, h, h, bias, tri)


def _mlstm_step_kernel(q_ref, k_ref, vc_ref, li_ref, lfp_ref, c0_ref, n0_ref, m0_ref,
                       h_ref, c1_ref, n1_ref, m1_ref):
    q = q_ref[...]
    k = k_ref[...] * (ML_D ** -0.5)
    vc = vc_ref[...]
    li = li_ref[...]
    lf = _log_sigmoid(lfp_ref[...])
    c0, n0, m0 = c0_ref[...], n0_ref[...], m0_ref[...]
    a = lf + m0
    m = jnp.maximum(a, li)
    w = jnp.exp(li - m)
    ws = jnp.exp(a - m)
    s = jnp.sum(q * k, axis=2, keepdims=True) * w
    cq = jnp.sum(c0 * q, axis=2, keepdims=True)
    num = s * vc + cq * ws
    den = s + ws * jnp.sum(n0 * q, axis=2, keepdims=True)
    h_ref[...] = num / jnp.maximum(jnp.abs(den), jnp.exp(-m))
    c1_ref[...] = ws * c0 + w * (vc * k)
    n1_ref[...] = ws * n0 + w * k
    m1_ref[...] = m


def mlstm_step(q, k, vc, li, lfp, c0, n0, m0):
    g_all = q.shape[0]
    g = 16

    def spec(shape):
        return pl.BlockSpec((g,) + shape, lambda i: (i, 0, 0))

    return pl.pallas_call(
        _mlstm_step_kernel,
        grid=(g_all // g,),
        in_specs=[spec((1, ML_D)), spec((1, ML_D)), spec((ML_D, 1)), spec((1, 1)), spec((1, 1)),
                  spec((ML_D, ML_D)), spec((1, ML_D)), spec((1, 1))],
        out_specs=[spec((ML_D, 1)), spec((ML_D, ML_D)), spec((1, ML_D)), spec((1, 1))],
        out_shape=[jax.ShapeDtypeStruct((g_all, ML_D, 1), F32),
                   jax.ShapeDtypeStruct((g_all, ML_D, ML_D), F32),
                   jax.ShapeDtypeStruct((g_all, 1, ML_D), F32),
                   jax.ShapeDtypeStruct((g_all, 1, 1), F32)],
        compiler_params=_cparams(("parallel",)),
        name="mlstm_step",
    )(q, k, vc, li, lfp, c0, n0, m0)


def _mm_res_kernel(a_ref, b_ref, r_ref, o_ref):
    o_ref[...] = r_ref[...] + _dot(a_ref[...], b_ref[...])


def matmul_residual(a, b, res, tm):
    m, kk = a.shape
    n = b.shape[1]
    return pl.pallas_call(
        _mm_res_kernel,
        grid=(m // tm,),
        in_specs=[pl.BlockSpec((tm, kk), lambda i: (i, 0)),
                  pl.BlockSpec((kk, n), lambda i: (0, 0)),
                  pl.BlockSpec((tm, n), lambda i: (i, 0))],
        out_specs=pl.BlockSpec((tm, n), lambda i: (i, 0)),
        out_shape=jax.ShapeDtypeStruct((m, n), F32),
        compiler_params=_cparams(("parallel",)),
        name="merge_matmul",
    )(a, b, res)


def _merge_kernel(cb_ref, cc_ref, ch_ref, mo_ref, hcc_ref, hch_ref, o_ref, hml_ref, cw_ref, gml_ref, w_ref, x_ref,
                  out_ref, *, tiles_per_seq):
    i = pl.program_id(0)
    u = cc_ref[...] * ch_ref[...]
    hu = jnp.where(i % tiles_per_seq == 0, 0.0, hcc_ref[...] * hch_ref[...])
    row = lax.broadcasted_iota(jnp.int32, u.shape, 0)
    prev1 = jnp.where(row == 0, hu[7:8], pltpu.roll(u, 1, 0))
    prev2 = jnp.where(row == 0, hu[6:7], jnp.where(row == 1, hu[7:8], pltpu.roll(u, 2, 0)))
    cw = cw_ref[...]
    conv = cw[0:1] * prev2 + cw[1:2] * prev1 + cw[2:3] * u
    a = (cb_ref[...] * conv).astype(BF16)

    hml = hml_ref[...]
    lane = lax.broadcasted_iota(jnp.int32, hml.shape, 1)
    sq = hml * hml
    scale = jnp.zeros_like(hml)
    for hh in range(ML_HEADS):
        seg = (lane >= hh * ML_D) & (lane < (hh + 1) * ML_D)
        ms = jnp.sum(jnp.where(seg, sq, 0.0), axis=1, keepdims=True) * (1.0 / ML_D)
        scale = jnp.where(seg, lax.rsqrt(ms + RMS_EPS), scale)
    ml = (hml * scale * gml_ref[...] * jax.nn.sigmoid(mo_ref[...])).astype(BF16)

    out_ref[...] = (x_ref[...] + _dot(a, w_ref[0:CONV_W, :])
                    + _dot(o_ref[...].astype(BF16), w_ref[CONV_W:CONV_W + MLA_W, :])
                    + _dot(ml, w_ref[CONV_W + MLA_W:, :]))


def merge_prompt(h, o_mla, h_ml, cw8, g_ml_row, w_out, x, tm, seq):
    m = x.shape[0]
    nb = seq // tm
    mo_blk = (3 * CONV_W + Q_RANK + KV_RANK + 3 * ML_W) // ML_W
    hb = tm // 8

    def col(cb):
        return pl.BlockSpec((tm, CONV_W), lambda i: (i, cb))

    def halo(cb):
        return pl.BlockSpec((8, CONV_W), lambda i: (jnp.maximum(i * hb - 1, 0), cb))

    return pl.pallas_call(
        functools.partial(_merge_kernel, tiles_per_seq=nb),
        grid=(m // tm,),
        in_specs=[col(0), col(1), col(2), col(mo_blk), halo(1), halo(2),
                  pl.BlockSpec((tm, MLA_W), lambda i: (i, 0)),
                  pl.BlockSpec((tm, ML_W), lambda i: (i, 0)),
                  pl.BlockSpec((8, CONV_W), lambda i: (0, 0)),
                  pl.BlockSpec((1, ML_W), lambda i: (0, 0)),
                  pl.BlockSpec((D_MODEL, D_MODEL), lambda i: (0, 0)),
                  pl.BlockSpec((tm, D_MODEL), lambda i: (i, 0))],
        out_specs=pl.BlockSpec((tm, D_MODEL), lambda i: (i, 0)),
        out_shape=jax.ShapeDtypeStruct((m, D_MODEL), F32),
        compiler_params=_cparams(("parallel",)),
        name="merge_prompt",
    )(h, h, h, h, h, h, o_mla, h_ml, cw8, g_ml_row, w_out, x)


def _bmm_kernel(a_ref, b_ref, o_ref):
    o_ref[...] = _dot(a_ref[...], b_ref[...])


def bmm(a, b):
    nb, m, kk = a.shape
    n = b.shape[2]
    return pl.pallas_call(
        _bmm_kernel,
        grid=(nb,),
        in_specs=[pl.BlockSpec((None, m, kk), lambda i: (i, 0, 0)),
                  pl.BlockSpec((None, kk, n), lambda i: (i, 0, 0))],
        out_specs=pl.BlockSpec((None, m, n), lambda i: (i, 0, 0)),
        out_shape=jax.ShapeDtypeStruct((nb, m, n), F32),
        compiler_params=_cparams(("parallel",)),
        name="bmm",
    )(a, b)


def _decode_kernel(pt_ref, ckv_hbm, kr_hbm, qlat_ref, qrope_ref, cnew_ref, rnew_ref, wukt_ref, o_ref,
                   cbuf, rbuf, sem, m_s, l_s, acc_s, *, layer):
    npg = DEC_PAGES
    nch = npg // DEC_CHUNK
    b, j = pl.program_id(0), pl.program_id(1)
    nj = pl.num_programs(1)
    step = b * nj + j
    last_step = pl.num_programs(0) * nj - 1
    slot = step % 2

    def page_copies(at_step, sl):
        bb, jj = at_step // nj, at_step % nj
        cps = []
        for k in range(npg):
            page = pt_ref[bb, jj * npg + k]
            cps.append(pltpu.make_async_copy(ckv_hbm.at[layer, page],
                                             cbuf.at[sl, pl.ds(k * PAGE_SIZE, PAGE_SIZE)], sem.at[0, sl]))
            cps.append(pltpu.make_async_copy(kr_hbm.at[layer, page], rbuf.at[sl, k], sem.at[1, sl]))
        return cps

    @pl.when(step == 0)
    def _():
        for cp in page_copies(step, slot):
            cp.start()

    for cp in page_copies(step, slot):
        cp.wait()
    prefetch = page_copies(jnp.minimum(step + 1, last_step), 1 - slot)
    per_chunk = 2 * len(prefetch) // nch

    @pl.when(j == 0)
    def _():
        m_s[...] = jnp.full_like(m_s, -jnp.inf)
        l_s[...] = jnp.zeros_like(l_s)
        acc_s[...] = jnp.zeros_like(acc_s)

    lhs = jnp.concatenate([wukt_ref[...], qlat_ref[...]], axis=0)
    qrope = qrope_ref[...]
    nk = MLA_HEADS * NOPE_D

    def products(c, rt):
        cb = c.astype(BF16)
        big = _dot_nt(lhs, cb)
        return cb, big, _dot(qrope, rt.astype(BF16)), jnp.sum(rt * rt, axis=0, keepdims=True)

    def weights(prod, valid, m_old):
        cb, big, s_rope, ssq_r = prod
        n = cb.shape[0]
        ssq = jnp.sum(jnp.square(big[:nk].reshape(NOPE_D, MLA_HEADS, n)), axis=0) + ssq_r
        s = (big[nk:nk + MLA_HEADS] + s_rope[:MLA_HEADS]) * lax.rsqrt(ssq * (1.0 / QK_D) + RMS_EPS)
        if valid is not None:
            s = jnp.where(valid, s, -jnp.inf)
        m_new = jnp.maximum(m_old, jnp.max(s, axis=1, keepdims=True))
        return m_new, jnp.exp2(m_old - m_new), jnp.exp2(s - m_new)

    def accumulate(w, cb, l_old, acc):
        _, alpha, p = w
        return alpha * l_old + jnp.sum(p, axis=1, keepdims=True), alpha * acc + _dot(p.astype(BF16), cb)

    rows = DEC_CHUNK * PAGE_SIZE

    def chunk_products(i):
        c = cbuf[slot, pl.ds(i * rows, rows), :]
        rt = jnp.concatenate([rbuf[slot, i * DEC_CHUNK + k] for k in range(DEC_CHUNK)], axis=1)
        return products(c, rt)

    m_run, l_run, acc = m_s[...], l_s[...], acc_s[...]
    prods = [chunk_products(i) for i in range(min(2, nch))]
    for i in range(nch):
        for cp in prefetch[i * per_chunk:(i + 1) * per_chunk]:
            cp.start()
        w = weights(prods[i], None, m_run)
        if i + 2 < nch:
            prods.append(chunk_products(i + 2))
        l_run, acc = accumulate(w, prods[i][0], l_run, acc)
        m_run = w[0]
    m_s[...], l_s[...], acc_s[...] = m_run, l_run, acc

    @pl.when(j == nj - 1)
    def _():
        lane = lax.broadcasted_iota(jnp.int32, (MLA_HEADS, PAGE_SIZE), 1)
        prod = products(cnew_ref[...], rnew_ref[...])
        l_fin, acc_fin = accumulate(weights(prod, lane == 0, m_run), prod[0], l_run, acc)
        o_ref[...] = acc_fin / l_fin

    @pl.when(step == last_step)
    def _():
        for cp in prefetch:
            cp.wait()


def decode_attention(layer, page_table, cache_ckv, cache_krope_t, qlat, qrope, cnew, rnew_t, wukt):
    nseq, n_pages = page_table.shape
    npg = DEC_PAGES
    nj = n_pages // npg

    in_specs = [pl.BlockSpec(memory_space=pl.ANY),
                pl.BlockSpec(memory_space=pl.ANY),
                pl.BlockSpec((None, Q_ROWS, KV_RANK), lambda b, j, pt: (b, 0, 0)),
                pl.BlockSpec((None, Q_ROWS, ROPE_D), lambda b, j, pt: (b, 0, 0)),
                pl.BlockSpec((None, PAGE_SIZE, KV_RANK), lambda b, j, pt: (b, 0, 0)),
                pl.BlockSpec((None, ROPE_D, PAGE_SIZE), lambda b, j, pt: (b, 0, 0)),
                pl.BlockSpec((MLA_HEADS * NOPE_D, KV_RANK), lambda b, j, pt: (0, 0))]
    return pl.pallas_call(
        functools.partial(_decode_kernel, layer=layer),
        grid_spec=pltpu.PrefetchScalarGridSpec(
            num_scalar_prefetch=1,
            grid=(nseq, nj),
            in_specs=in_specs,
            out_specs=pl.BlockSpec((None, MLA_HEADS, KV_RANK), lambda b, j, pt: (b, 0, 0)),
            scratch_shapes=[pltpu.VMEM((2, npg * PAGE_SIZE, KV_RANK), F32),
                            pltpu.VMEM((2, npg, ROPE_D, PAGE_SIZE), F32),
                            pltpu.SemaphoreType.DMA((2, 2)),
                            pltpu.VMEM((MLA_HEADS, 1), F32),
                            pltpu.VMEM((MLA_HEADS, 1), F32),
                            pltpu.VMEM((MLA_HEADS, KV_RANK), F32)]),
        out_shape=jax.ShapeDtypeStruct((nseq, MLA_HEADS, KV_RANK), F32),
        compiler_params=_cparams(("arbitrary", "arbitrary")),
        name="decode_attention",
    )(page_table, cache_ckv, cache_krope_t, qlat, qrope, cnew, rnew_t, wukt)


def _ffn_kernel(x_ref, g_ref, w1_ref, w3_ref, w2_ref, o_ref, xn_ref):
    f = pl.program_id(1)

    @pl.when(f == 0)
    def _():
        x = x_ref[...]
        xn_ref[...] = _rms(x, g_ref[...]).astype(BF16)
        o_ref[...] = x

    xn = xn_ref[...]
    a = _dot(xn, w1_ref[...])
    b = _dot(xn, w3_ref[...])
    act = (a * jax.nn.sigmoid(a) * b).astype(BF16)
    o_ref[...] += _dot(act, w2_ref[...])


def ffn_dense(x, g, w1, w3, w2, tm, tf):
    m = x.shape[0]
    nf = w1.shape[1] // tf
    return pl.pallas_call(
        _ffn_kernel,
        grid=(m // tm, nf),
        in_specs=[pl.BlockSpec((tm, D_MODEL), lambda i, f: (i, 0)),
                  pl.BlockSpec((1, D_MODEL), lambda i, f: (0, 0)),
                  pl.BlockSpec((D_MODEL, tf), lambda i, f: (0, f)),
                  pl.BlockSpec((D_MODEL, tf), lambda i, f: (0, f)),
                  pl.BlockSpec((tf, D_MODEL), lambda i, f: (f, 0))],
        out_specs=pl.BlockSpec((tm, D_MODEL), lambda i, f: (i, 0)),
        out_shape=jax.ShapeDtypeStruct((m, D_MODEL), F32),
        scratch_shapes=[pltpu.VMEM((tm, D_MODEL), BF16)],
        compiler_params=_cparams(("parallel", "arbitrary")),
        name="ffn_dense",
    )(x, g, w1, w3, w2)


def _router_kernel(x_ref, g_ref, wh_ref, wl_ref, info_ref):
    xn = _rms(x_ref[...], g_ref[...])
    hi, lo = _split2(xn)
    lg =_dot(hi, wh_ref[...]) + _dot(hi, wl_ref[...]) + _dot(lo, wh_ref[...])
    lane = lax.broadcasted_iota(jnp.int32, lg.shape, 1).astype(F32)
    lg = jnp.where(lane < N_EXPERTS, lg, -jnp.inf)
    v1 = jnp.max(lg, axis=1, keepdims=True)
    i1 = jnp.min(jnp.where(lg == v1, lane, float(LANE)), axis=1, keepdims=True)
    rest = jnp.where(lane == i1, -jnp.inf, lg)
    v2 = jnp.max(rest, axis=1, keepdims=True)
    i2 = jnp.min(jnp.where(rest == v2, lane, float(LANE)), axis=1, keepdims=True)
    e = jnp.exp(v2 - v1)
    g1 = 1.0 / (1.0 + e)
    info_ref[...] = jnp.where(lane == 0, i1, jnp.where(lane == 1, i2, jnp.where(lane == 2, g1, e * g1)))


def router(x, g, wr_hi, wr_lo, tm):
    m = x.shape[0]
    return pl.pallas_call(
        _router_kernel,
        grid=(m // tm,),
        in_specs=[pl.BlockSpec((tm, D_MODEL), lambda i: (i, 0)),
                  pl.BlockSpec((1, D_MODEL), lambda i: (0, 0)),
                  pl.BlockSpec((D_MODEL, LANE), lambda i: (0, 0)),
                  pl.BlockSpec((D_MODEL, LANE), lambda i: (0, 0))],
        out_specs=pl.BlockSpec((tm, LANE), lambda i: (i, 0)),
        out_shape=jax.ShapeDtypeStruct((m, LANE), F32),
        compiler_params=_cparams(("parallel",)),
        name="router",
    )(x, g, wr_hi, wr_lo)


def _moe_block_kernel(be_ref, nu_ref, x_ref, g_ref, gate_ref, w1_ref, w3_ref, w2_ref, o_ref):
    del be_ref
    i = pl.program_id(0)

    @pl.when(i < nu_ref[0])
    def _():
        x = _rms(x_ref[...], g_ref[...]).astype(BF16)
        a = _dot(x, w1_ref[...])
        b = _dot(x, w3_ref[...])
        act = (a * jax.nn.sigmoid(a) * b).astype(BF16)
        o_ref[...] = _dot(act, w2_ref[...]) * gate_ref[...]

    @pl.when(i >= nu_ref[0])
    def _():
        o_ref[...] = jnp.zeros_like(o_ref)


def moe_blocks(blk_exp, n_used, xg, g, slot_gate, w1, w3, w2):
    n_slot = xg.shape[0]
    tm = MOE_TM
    return pl.pallas_call(
        _moe_block_kernel,
        grid_spec=pltpu.PrefetchScalarGridSpec(
            num_scalar_prefetch=2,
            grid=(n_slot // tm,),
            in_specs=[pl.BlockSpec((tm, D_MODEL), lambda i, be, nu: (i, 0)),
                      pl.BlockSpec((1, D_MODEL), lambda i, be, nu: (0, 0)),
                      pl.BlockSpec((tm, 1), lambda i, be, nu: (i, 0)),
                      pl.BlockSpec((None, D_MODEL, D_FF_EXPERT), lambda i, be, nu: (be[i], 0, 0)),
                      pl.BlockSpec((None, D_MODEL, D_FF_EXPERT), lambda i, be, nu: (be[i], 0, 0)),
                      pl.BlockSpec((None, D_FF_EXPERT, D_MODEL), lambda i, be, nu: (be[i], 0, 0))],
            out_specs=pl.BlockSpec((tm, D_MODEL), lambda i, be, nu: (i, 0))),
        out_shape=jax.ShapeDtypeStruct((n_slot, D_MODEL), F32),
        compiler_params=_cparams(("arbitrary",)),
        name="moe_blocks",
    )(blk_exp, n_used, xg, g, slot_gate, w1, w3, w2)


def _cast_kernel(x_ref, o_ref):
    o_ref[...] = x_ref[...].astype(o_ref.dtype)


def cast_experts_bf16(w):
    e, r, c = w.shape
    spec = pl.BlockSpec((None, r, c), lambda i: (i, 0, 0))
    return pl.pallas_call(
        _cast_kernel,
        grid=(e,),
        in_specs=[spec],
        out_specs=spec,
        out_shape=jax.ShapeDtypeStruct(w.shape, BF16),
        compiler_params=_cparams(("parallel",)),
        name="cast_experts",
    )(w)


def _add3_kernel(a_ref, b_ref, c_ref, o_ref):
    o_ref[...] = a_ref[...] + b_ref[...] + c_ref[...]


def add3(a, b, c, tm):
    m, n = a.shape
    spec = pl.BlockSpec((tm, n), lambda i: (i, 0))
    return pl.pallas_call(
        _add3_kernel,
        grid=(m // tm,),
        in_specs=[spec, spec, spec],
        out_specs=spec,
        out_shape=jax.ShapeDtypeStruct((m, n), F32),
        compiler_params=_cparams(("parallel",)),
        name="moe_combine",
    )(a, b, c)


def _moe_dense_kernel(x_ref, g_ref, gate_ref, w1_ref, w3_ref, w2_ref, o_ref):
    e = pl.program_id(0)

    @pl.when(e == 0)
    def _():
        o_ref[...] = x_ref[...]

    xn = _rms(x_ref[...], g_ref[...]).astype(BF16)
    a = _dot(xn, w1_ref[...])
    b = _dot(xn, w3_ref[...])
    act = (a * jax.nn.sigmoid(a) * b).astype(BF16)
    o_ref[...] += _dot(act, w2_ref[...]) * gate_ref[...]


def moe_dense(x, g, gates, w1, w3, w2):
    m = x.shape[0]
    return pl.pallas_call(
        _moe_dense_kernel,
        grid=(N_EXPERTS,),
        in_specs=[pl.BlockSpec((m, D_MODEL), lambda e: (0, 0)),
                  pl.BlockSpec((1, D_MODEL), lambda e: (0, 0)),
                  pl.BlockSpec((None, m, 1), lambda e: (e, 0, 0)),
                  pl.BlockSpec((None, D_MODEL, D_FF_EXPERT), lambda e: (e, 0, 0)),
                  pl.BlockSpec((None, D_MODEL, D_FF_EXPERT), lambda e: (e, 0, 0)),
                  pl.BlockSpec((None, D_FF_EXPERT, D_MODEL), lambda e: (e, 0, 0))],
        out_specs=pl.BlockSpec((m, D_MODEL), lambda e: (0, 0)),
        out_shape=jax.ShapeDtypeStruct((m, D_MODEL), F32),
        compiler_params=_cparams(("arbitrary",)),
        name="moe_dense",
    )(x, g, gates, w1, w3, w2)


def _rope_tables(pos, rope_lane0):
    inv_freq = ROPE_THETA ** (-jnp.arange(HALF_ROPE, dtype=F32) / HALF_ROPE)
    ang = pos.astype(F32)[:, None] * inv_freq[None, :]
    cos, sin = jnp.cos(ang), jnp.sin(ang)
    n = pos.shape[0]
    z = lambda w: jnp.zeros((n, w), F32)
    tail = LANE - rope_lane0 - ROPE_D
    c = jnp.concatenate([jnp.ones((n, rope_lane0), F32), cos, cos, z(tail)], axis=1)
    s1 = jnp.concatenate([z(rope_lane0), -sin, z(HALF_ROPE), z(tail)], axis=1)
    s2 = jnp.concatenate([z(rope_lane0), z(HALF_ROPE), sin, z(tail)], axis=1)
    return c, s1, s2


def _head_gain(g):
    return jnp.concatenate([g[:NOPE_D], g[NOPE_D:], g[NOPE_D:], jnp.zeros((HP - QK_D,), F32)])[None, :]


def _prep_layer(l, w_in, g_cq, w_uq, g_ckv, w_uk, w_uv, g_q, g_k, b_i, b_f, w_out):
    d = {}
    wi = w_in[l]
    kr0 = 3 * CONV_W + Q_RANK + KV_RANK
    d["w_in"] = jnp.concatenate(
        [wi[:, :kr0], wi[:, kr0 + ROPE_D:kr0 + ROPE_D + 4 * ML_W], wi[:, kr0:kr0 + ROPE_D],
         wi[:, kr0 + ROPE_D + 4 * ML_W:], jnp.zeros((D_MODEL, LANE - ROPE_D - 2 * ML_HEADS), F32)],
        axis=1).astype(BF16)
    wq = w_uq[l].reshape(Q_RANK, MLA_HEADS, QK_D)
    d["w_uq"] = jnp.pad(wq, ((0, 0), (0, 0), (0, HP - QK_D))).reshape(Q_RANK, MLA_HEADS * HP).astype(BF16)
    wk = jnp.pad(w_uk[l], ((0, 0), (0, 0), (0, HP - NOPE_D))).reshape(KV_RANK, MLA_HEADS * HP)
    place = jnp.zeros((LANE, MLA_HEADS, HP), F32)
    place = place.at[jnp.arange(ROPE_D)[:, None], jnp.arange(MLA_HEADS)[None, :],
                     NOPE_D + jnp.arange(ROPE_D)[:, None]].set(1.0)
    d["w_k"] = jnp.concatenate([wk, place.reshape(LANE, MLA_HEADS * HP)], axis=0).astype(BF16)
    d["w_vt"] = w_uv[l].reshape(KV_RANK, MLA_W).T.astype(BF16)
    d["w_ukt_dec"] = w_uk[l].transpose(2, 1, 0).reshape(NOPE_D * MLA_HEADS, KV_RANK).astype(BF16)
    d["w_uk_h"] = w_uk[l].transpose(1, 2, 0).astype(BF16)
    d["w_uv_h"] = w_uv[l].transpose(1, 0, 2).astype(BF16)
    d["g_cq"] = g_cq[l][None, :]
    d["g_ckv"] = g_ckv[l][None, :]
    d["g_q"] = _head_gain(g_q[l])
    d["g_k"] = _head_gain(g_k[l])
    bias = jnp.zeros((LANE,), F32).at[GATE_I:GATE_I + ML_HEADS].set(b_i[l])
    d["gate_bias"] = bias.at[GATE_F:GATE_F + ML_HEADS].set(b_f[l])[None, :]
    d["w_out"] = w_out[l].astype(BF16)
    return d


def _split_h(h):
    o = 0
    out = []
    for w in (CONV_W, CONV_W, CONV_W, Q_RANK, KV_RANK, ML_W, ML_W, ML_W, ML_W):
        out.append(h[:, o:o + w])
        o += w
    return out


def _ml_merge(h_ml, mo, g_ml_l):
    hm = h_ml.reshape(-1, ML_HEADS, ML_D)
    ml = hm * lax.rsqrt(jnp.mean(hm * hm, axis=-1, keepdims=True) + RMS_EPS) * g_ml_l[None]
    return ml.reshape(-1, ML_W) * jax.nn.sigmoid(mo)


def _route(info):
    return info[:, TOP_K:2 * TOP_K], info[:, :TOP_K].astype(jnp.int32)


def kernel(x_prompt, x_sample, cache_ckv, cache_krope, page_table, state_conv, state_C, state_n, state_m,
           g_norm_mix, g_norm_ffn, w_in, conv_w, g_cq, w_uq, g_ckv, w_uk, w_uv, g_q, g_k, b_i, b_f, g_ml,
           w_out, w_ff1, w_ff3, w_ff2, w_router, w_e1, w_e3, w_e2):
    batch, seq, _ = x_prompt.shape
    nseq = x_sample.shape[0]
    depth = w_in.shape[0]
    past_len = page_table.shape[1] * PAGE_SIZE
    mp = batch * seq
    tm_p = 512
    assert seq % max(tm_p, ATTN_T, ML_L) == 0 and x_sample.shape[1] == 1
    assert page_table.shape[1] % DEC_PAGES == 0 and DEC_PAGES % (2 * DEC_CHUNK) == 0
    assert nseq % 8 == 0 and (nseq * ML_HEADS) % 16 == 0

    xp = x_prompt.reshape(mp, D_MODEL)
    xs = x_sample.reshape(nseq, D_MODEL)

    pos_p = jnp.arange(seq)
    pos_s = jnp.full((nseq,), past_len)
    tabs_q_p = _rope_tables(pos_p, NOPE_D)
    tabs_k_p = _rope_tables(pos_p, 0)
    tabs_q_s = _rope_tables(pos_s, NOPE_D)
    tabs_k_s = _rope_tables(pos_s, 0)
    tri = jnp.tril(jnp.ones((ML_L, ML_L), F32)).astype(BF16)
    scale_row = jnp.full((1, HP), ATTN_SCALE * LOG2E, F32)
    cache_krope_t = cache_krope.transpose(0, 1, 3, 2)

    outs = {k: [] for k in ("ckv_p", "ckv_s", "kr_p", "kr_s", "cv_p", "cv_s",
                            "C_p", "C_s", "n_p", "n_s", "m_p", "m_s")}

    for l in range(depth):
        p = _prep_layer(l, w_in, g_cq, w_uq, g_ckv, w_uk, w_uv, g_q, g_k, b_i, b_f, w_out)
        gmix = g_norm_mix[l][None, :]
        gffn = g_norm_ffn[l][None, :]
        cw = conv_w[l]

        h = inproj(xp, gmix, p["w_in"], tm_p)
        q = q_path(h, p["g_cq"], p["w_uq"], p["g_q"] * scale_row, tabs_q_p, tm_p, seq // tm_p)
        ckvn, krr = ckv_path(h, p["g_ckv"], tabs_k_p, tm_p, seq // tm_p)
        k, vt = kv_path(ckvn, krr, p["w_k"], p["w_vt"], p["g_k"], batch, seq)
        o_mla = prompt_attention(q.reshape(batch, seq, -1), k.reshape(batch, seq, -1), vt, batch, seq)
        h_ml, c1, n1, m1 = mlstm_prompt(h, p["gate_bias"], tri, batch, seq)

        cw8 = jnp.pad(cw, ((0, 8 - CONV_K), (0, 0)))
        xp = merge_prompt(h, o_mla.reshape(mp, MLA_W), h_ml, cw8, g_ml[l].reshape(1, ML_W), p["w_out"], xp,
                          tm_p, seq)
        tail = h.reshape(batch, seq, IN_PAD)[:, seq - (CONV_K - 1):, CONV_W:3 * CONV_W]

        outs["ckv_p"].append(ckvn.reshape(batch, seq, KV_RANK))
        outs["kr_p"].append(krr[:, :ROPE_D].reshape(batch, seq, ROPE_D))
        outs["cv_p"].append(tail[..., :CONV_W] * tail[..., CONV_W:])
        outs["C_p"].append(c1)
        outs["n_p"].append(n1.reshape(batch, ML_HEADS, ML_D))
        outs["m_p"].append(m1.reshape(batch, ML_HEADS))

        hs = inproj(xs, gmix, p["w_in"], nseq)
        cb, cc, ch, _, _, mq, mk, mv, mo = _split_h(hs)
        gates = hs[:, IN_MAIN:]
        qs = q_path(hs, p["g_cq"], p["w_uq"], p["g_q"] * p["g_k"] * scale_row, tabs_q_s, nseq, 1)
        ckvn_s, krr_s = ckv_path(hs, p["g_ckv"], tabs_k_s, nseq, 1)
        qs3 = qs.reshape(nseq, MLA_HEADS, HP)
        qlat = bmm(qs3[:, :, :NOPE_D].transpose(1, 0, 2), p["w_uk_h"])
        qpad = ((0, 0), (0, Q_ROWS - MLA_HEADS), (0, 0))
        qlat = jnp.pad(qlat.transpose(1, 0, 2).astype(BF16), qpad)
        qrope = jnp.pad(qs3[:, :, NOPE_D:QK_D], qpad)
        cnew = jnp.pad(ckvn_s[:, None, :], ((0, 0), (0, PAGE_SIZE - 1), (0, 0)))
        rnew_t = jnp.pad(krr_s[:, :ROPE_D, None], ((0, 0), (0, 0), (0, PAGE_SIZE - 1)))
        pc = decode_attention(l, page_table, cache_ckv, cache_krope_t, qlat, qrope, cnew, rnew_t, p["w_ukt_dec"])
        o_s = bmm(pc.transpose(1, 0, 2).astype(BF16), p["w_uv_h"])
        o_s = o_s.transpose(1, 0, 2).reshape(nseq, MLA_W)

        g_all = nseq * ML_HEADS
        li = (gates[:, GATE_I:GATE_I + ML_HEADS] + b_i[l][None, :]).reshape(g_all, 1, 1)
        lfp = (gates[:, GATE_F:GATE_F + ML_HEADS] + b_f[l][None, :]).reshape(g_all, 1, 1)
        h_col, c1s, n1s, m1s = mlstm_step(
            mq.reshape(g_all, 1, ML_D), mk.reshape(g_all, 1, ML_D), mv.reshape(g_all, ML_D, 1), li, lfp,
            state_C[l].reshape(g_all, ML_D, ML_D), state_n[l].reshape(g_all, 1, ML_D),
            state_m[l].reshape(g_all, 1, 1))

        us = cc * ch
        st = state_conv[l]
        conv_s = cw[0] * st[:, 0] + cw[1] * st[:, 1] + cw[2] * us
        mix = jnp.concatenate([cb * conv_s, o_s, _ml_merge(h_col.reshape(nseq, ML_W), mo, g_ml[l])], axis=1)
        xs = matmul_residual(mix.astype(BF16), p["w_out"], xs, nseq)

        outs["ckv_s"].append(ckvn_s[:, None, :])
        outs["kr_s"].append(krr_s[:, None, :ROPE_D])
        outs["cv_s"].append(jnp.stack([st[:, 1], us], axis=1))
        outs["C_s"].append(c1s.reshape(nseq, ML_HEADS, ML_D, ML_D))
        outs["n_s"].append(n1s.reshape(nseq, ML_HEADS, ML_D))
        outs["m_s"].append(m1s.reshape(nseq, ML_HEADS))

        j = l // 2
        if l % 2 == 0:
            w1, w3, w2 = w_ff1[j].astype(BF16), w_ff3[j].astype(BF16), w_ff2[j].astype(BF16)
            xp = ffn_dense(xp, gffn, w1, w3, w2, tm_p, D_FF // 2)
            xs = ffn_dense(xs, gffn, w1, w3, w2, nseq, D_FF // 2)
        else:
            w1, w3, w2 = cast_experts_bf16(w_e1[j]), cast_experts_bf16(w_e3[j]), cast_experts_bf16(w_e2[j])
            wr = jnp.pad(w_router[j], ((0, 0), (0, LANE - N_EXPERTS)))
            wr_hi = wr.astype(BF16)
            wr_lo = (wr - wr_hi.astype(F32)).astype(BF16)

            gate, top_i = _route(router(xp, gffn, wr_hi, wr_lo, tm_p))
            e_flat = top_i.reshape(-1)
            n_assign = mp * TOP_K
            onehot = (e_flat[:, None] == jnp.arange(N_EXPERTS)[None, :]).astype(jnp.int32)
            rank = jnp.take_along_axis(jnp.cumsum(onehot, axis=0) - onehot, e_flat[:, None], axis=1)[:, 0]
            counts = jnp.sum(onehot, axis=0)
            padded = (counts + MOE_TM - 1) // MOE_TM * MOE_TM
            pad_end = jnp.cumsum(padded)
            pad_start = pad_end - padded
            dest = pad_start[e_flat] + rank
            n_blk = (n_assign + N_EXPERTS * (MOE_TM - 1) + MOE_TM - 1) // MOE_TM
            n_slot = n_blk * MOE_TM
            slot_asg = jnp.full((n_slot,), -1, jnp.int32).at[dest].set(jnp.arange(n_assign, dtype=jnp.int32))
            slot_tok = jnp.maximum(slot_asg, 0) // TOP_K
            slot_gate = jnp.where(slot_asg >= 0, gate.reshape(-1)[jnp.maximum(slot_asg, 0)], 0.0)
            blk_start = jnp.arange(n_blk, dtype=jnp.int32) * MOE_TM
            blk_exp = jnp.minimum(jnp.sum((pad_end[None, :] <= blk_start[:, None]).astype(jnp.int32), axis=1),
                                  N_EXPERTS - 1)
            n_used = (pad_end[-1:] // MOE_TM).astype(jnp.int32)
            yb = moe_blocks(blk_exp, n_used, xp[slot_tok], gffn, slot_gate[:, None], w1, w3, w2)
            dest2 = dest.reshape(mp, TOP_K)
            xp = add3(xp, yb[dest2[:, 0]], yb[dest2[:, 1]], tm_p)

            gate_s, top_s = _route(router(xs, gffn, wr_hi, wr_lo, nseq))
            dense_gate = jnp.sum(
                (top_s[:, :, None] == jnp.arange(N_EXPERTS)[None, None, :]) * gate_s[:, :, None], axis=1)
            xs = moe_dense(xs, gffn, dense_gate.T[:, :, None], w1, w3, w2)

    return (xp.reshape(batch, seq, D_MODEL), xs.reshape(nseq, 1, D_MODEL),
            jnp.stack(outs["ckv_p"]), jnp.stack(outs["ckv_s"]),
            jnp.stack(outs["kr_p"]), jnp.stack(outs["kr_s"]),
            jnp.stack(outs["cv_p"]), jnp.stack(outs["cv_s"]),
            jnp.stack(outs["C_p"]), jnp.stack(outs["C_s"]),
            jnp.stack(outs["n_p"]), jnp.stack(outs["n_s"]),
            jnp.stack(outs["m_p"]), jnp.stack(outs["m_s"]))
```
